```python
import math
import jax, jax.numpy as jnp
from jax import lax
import numpy as np

D_MODEL = 4096
BATCH = 2
SEQ = 8192
DEPTH = 1

CONV_CH = D_MODEL // 2
CONV_KERNEL = 31
N_HEADS = 16
HEAD_DIM = 128
ATTN_WIDTH = N_HEADS * HEAD_DIM
IDX_HEADS = 32
IDX_DIM = 64
TOPK_MAX = 256
Q_BLOCK = 128
N_BUCKETS = 32
MAX_EXACT = N_BUCKETS // 2
MAX_DISTANCE = 128
D_FF = 4 * D_MODEL
N_BRANCHES = 2
EPS = 1e-6

OFF_CONV_A = 0
OFF_CONV_G = OFF_CONV_A + CONV_CH
OFF_Q = OFF_CONV_G + CONV_CH
OFF_K = OFF_Q + ATTN_WIDTH
OFF_V = OFF_K + ATTN_WIDTH
OFF_IQ = OFF_V + ATTN_WIDTH
OFF_IK = OFF_IQ + IDX_HEADS * IDX_DIM
OFF_IW = OFF_IK + IDX_DIM
OFF_GATE = OFF_IW + IDX_HEADS
IN_WIDTH = OFF_GATE + N_BRANCHES * D_MODEL

kernel_name = "hybrid_conformer_conv_dsa_gated_block"


def _rmsnorm(x, g):
    xf = x.astype(jnp.float32)
    y = xf * lax.rsqrt(jnp.mean(xf * xf, axis=-1, keepdims=True) + EPS)
    return (y * g.astype(jnp.float32)).astype(x.dtype)


def _layernorm(x, g, b):
    xf = x.astype(jnp.float32)
    mu = jnp.mean(xf, axis=-1, keepdims=True)
    xc = xf - mu
    y = xc * lax.rsqrt(jnp.mean(xc * xc, axis=-1, keepdims=True) + EPS)
    return (y * g.astype(jnp.float32) + b.astype(jnp.float32)).astype(x.dtype)


def _t5_bucket(n):
    nf = jnp.maximum(n, 1).astype(jnp.float32)
    large = MAX_EXACT + (jnp.log(nf / MAX_EXACT) / math.log(MAX_DISTANCE / MAX_EXACT)
                         * (N_BUCKETS - MAX_EXACT)).astype(jnp.int32)
    large = jnp.minimum(large, N_BUCKETS - 1)
    return jnp.where(n < MAX_EXACT, n, large)


def _conv_branch(a, g, conv_w, conv_bias, ln_g, ln_b, w_out):
    h = a * jax.nn.sigmoid(g)
    h = lax.conv_general_dilated(
        h, conv_w[:, None, :].astype(h.dtype), window_strides=(1,),
        padding=[(CONV_KERNEL - 1, 0)], dimension_numbers=('NWC', 'WIO', 'NWC'),
        feature_group_count=CONV_CH) + conv_bias
    h = jax.nn.silu(_layernorm(h, ln_g, ln_b))
    return h @ w_out


def _dsa_branch(q, k, v, qi, ki, wi, rel_bias, w_out):
    B, S = q.shape[0], q.shape[1]
    topk = min(TOPK_MAX, S // 4)
    nb = S // Q_BLOCK
    key_pos = jnp.arange(S, dtype=jnp.int32)
    wi = wi * (IDX_HEADS ** -0.5)

    def to_blocks(t):
        return jnp.moveaxis(t.reshape((B, nb, Q_BLOCK) + t.shape[2:]), 1, 0)

    def block(args):
        qb, qib, wib, start = args
        q_pos = start + jnp.arange(Q_BLOCK, dtype=jnp.int32)
        s = jnp.einsum('bqhd,bsd->bqhs', qib, ki) * (IDX_DIM ** -0.5)
        score = jnp.einsum('bqhs,bqh->bqs', jax.nn.relu(s), wib).astype(jnp.float32)
        causal = key_pos[None, :] <= q_pos[:, None]
        score = jnp.where(causal[None], score, -jnp.inf)
        _, sel = lax.top_k(score, topk)
        k_sel = jax.vmap(lambda kb, ib: kb[ib])(k, sel)
        v_sel = jax.vmap(lambda vb, ib: vb[ib])(v, sel)
        logits = jnp.einsum('bqhd,bqkhd->bhqk', qb, k_sel).astype(jnp.float32) * (HEAD_DIM ** -0.5)
        dist = q_pos[None, :, None] - sel
        bias = rel_bias[_t5_bucket(jnp.maximum(dist, 0))]
        logits = logits + jnp.transpose(bias, (0, 3, 1, 2)).astype(jnp.float32)
        logits = jnp.where((dist >= 0)[:, None], logits, -jnp.inf)
        p = jax.nn.softmax(logits, axis=-1).astype(v.dtype)
        return jnp.einsum('bhqk,bqkhd->bqhd', p, v_sel)

    starts = jnp.arange(nb, dtype=jnp.int32) * Q_BLOCK
    o = lax.map(block, (to_blocks(q), to_blocks(qi), to_blocks(wi), starts))
    o = jnp.moveaxis(o, 0, 1).reshape(B, S, ATTN_WIDTH)
    return o @ w_out


def setup_inputs(seed: int = 0) -> dict:
    key = jax.random.key(seed)
    ks = jax.random.split(key, 20)
    f32 = jnp.float32
    nrm = lambda k, shape, scale: jax.random.normal(k, shape, f32) * scale
    L = DEPTH
    return {
        "x": nrm(ks[0], (BATCH, SEQ, D_MODEL), 1.0),
        "norm1_g": 1.0 + nrm(ks[1], (L, D_MODEL), 0.05),
        "w_in": nrm(ks[2], (L, D_MODEL, IN_WIDTH), D_MODEL ** -0.5),
        "b_gate": nrm(ks[3], (L, N_BRANCHES * D_MODEL), 0.01),
        "conv_w": nrm(ks[4], (L, CONV_KERNEL, CONV_CH), CONV_KERNEL ** -0.5),
        "conv_bias": nrm(ks[5], (L, CONV_CH), 0.01),
        "conv_ln_g": 1.0 + nrm(ks[6], (L, CONV_CH), 0.05),
        "conv_ln_b": nrm(ks[7], (L, CONV_CH), 0.01),
        "w_conv_out": nrm(ks[8], (L, CONV_CH, D_MODEL), CONV_CH ** -0.5),
        "w_attn_out": nrm(ks[9], (L, ATTN_WIDTH, D_MODEL), ATTN_WIDTH ** -0.5),
        "rel_bias": nrm(ks[10], (N_BUCKETS, N_HEADS), 0.5),
        "w_o": nrm(ks[11], (L, D_MODEL, D_MODEL), D_MODEL ** -0.5),
        "norm2_g": 1.0 + nrm(ks[12], (L, D_MODEL), 0.05),
        "w_ff1": nrm(ks[13], (L, D_MODEL, D_FF), D_MODEL ** -0.5),
        "w_ff2": nrm(ks[14], (L, D_FF, D_MODEL), D_FF ** -0.5),
        "normf_g": 1.0 + nrm(ks[15], (D_MODEL,), 0.05),
    }


def reference(x, norm1_g, w_in, b_gate, conv_w, conv_bias, conv_ln_g, conv_ln_b,
              w_conv_out, w_attn_out, rel_bias, w_o, norm2_g, w_ff1, w_ff2, normf_g):
    B, S, _ = x.shape
    h = x
    for l in range(DEPTH):
        u = _rmsnorm(h, norm1_g[l])
        p = u @ w_in[l]
        conv_a = p[..., OFF_CONV_A:OFF_CONV_G]
        conv_g = p[..., OFF_CONV_G:OFF_Q]
        q = p[..., OFF_Q:OFF_K].reshape(B, S, N_HEADS, HEAD_DIM)
        k = p[..., OFF_K:OFF_V].reshape(B, S, N_HEADS, HEAD_DIM)
        v = p[..., OFF_V:OFF_IQ].reshape(B, S, N_HEADS, HEAD_DIM)
        qi = p[..., OFF_IQ:OFF_IK].reshape(B, S, IDX_HEADS, IDX_DIM)
        ki = p[..., OFF_IK:OFF_IW]
        wi = p[..., OFF_IW:OFF_GATE]
        gates = jax.nn.sigmoid(p[..., OFF_GATE:] + b_gate[l]).reshape(B, S, N_BRANCHES, D_MODEL)

        y_a = _conv_branch(conv_a, conv_g, conv_w[l], conv_bias[l], conv_ln_g[l], conv_ln_b[l], w_conv_out[l])
        y_b = _dsa_branch(q, k, v, qi, ki, wi, rel_bias, w_attn_out[l])
        mixed = gates[:, :, 0, :] * y_a + gates[:, :, 1, :] * y_b
        h = h + mixed @ w_o[l]

        u2 = _rmsnorm(h, norm2_g[l])
        h = h + jnp.square(jax.nn.relu(u2 @ w_ff1[l])) @ w_ff2[l]
    return _rmsnorm(h, normf_g)
```

```python
import functools
import math

import jax
import jax.numpy as jnp
from jax import lax
from jax.experimental import pallas as pl
from jax.experimental.pallas import tpu as pltpu

IDX_HEADS = 32
IDX_DIM = 64
TOPK_MAX = 256
N_BUCKETS = 32
MAX_DISTANCE = 128
EPS = 1e-6

V7X_VMEM_BYTES = 64 * 1024 * 1024
VMEM_LIMIT_BYTES = V7X_VMEM_BYTES - 8 * 1024 * 1024
LANES = 128
MXU_DIM = 256
MASKED = -1e30

F32 = jnp.float32
BF16 = jnp.bfloat16


def _params(*sem):
    return pltpu.CompilerParams(dimension_semantics=sem, vmem_limit_bytes=VMEM_LIMIT_BYTES)


def _rmsnorm_kernel(x_ref, g_ref, o_ref):
    x = x_ref[...]
    ms = jnp.mean(x * x, axis=-1, keepdims=True)
    o_ref[...] = (x * lax.rsqrt(ms + EPS) * g_ref[...]).astype(o_ref.dtype)


def _rmsnorm(x, g, out_dtype, tr=256):
    n, d = x.shape
    tr = min(tr, n)
    return pl.pallas_call(
        _rmsnorm_kernel,
        grid=(n // tr,),
        in_specs=[pl.BlockSpec((tr, d), lambda i: (i, 0)), pl.BlockSpec((1, d), lambda i: (0, 0))],
        out_specs=pl.BlockSpec((tr, d), lambda i: (i, 0)),
        out_shape=jax.ShapeDtypeStruct((n, d), out_dtype),
        compiler_params=_params("parallel"),
        name="rmsnorm",
    )(x, g.reshape(1, d))


def _mm_kernel(*refs, nk, n_extra, epilogue):
    a_ref, w_ref = refs[0], refs[1]
    extra = refs[2:2 + n_extra]
    o_ref = refs[2 + n_extra]
    if nk == 1:
        acc = jnp.dot(a_ref[...], w_ref[...], preferred_element_type=F32)
        o_ref[...] = epilogue(acc, *[e[...] for e in extra]).astype(o_ref.dtype)
        return
    acc_ref = refs[3 + n_extra]
    k = pl.program_id(2)

    @pl.when(k == 0)
    def _():
        acc_ref[...] = jnp.zeros_like(acc_ref)

    acc_ref[...] += jnp.dot(a_ref[...], w_ref[...], preferred_element_type=F32)

    @pl.when(k == nk - 1)
    def _():
        o_ref[...] = epilogue(acc_ref[...], *[e[...] for e in extra]).astype(o_ref.dtype)


def _matmul(a, w, out_dtype, *, epilogue=None, extras=(), tm=1024, tn=1024, tk=4096, name="matmul"):
    m, kd = a.shape
    n = w.shape[1]
    tm, tn, tk = min(tm, m), min(tn, n), min(tk, kd)
    nk = kd // tk
    if epilogue is None:
        epilogue = lambda acc: acc
    in_specs = [pl.BlockSpec((tm, tk), lambda i, j, k: (i, k)), pl.BlockSpec((tk, tn), lambda i, j, k: (k, j))]
    args = [a, w]
    for ex in extras:
        if ex[0] == "row":
            in_specs.append(pl.BlockSpec((1, tn), lambda i, j, k: (0, j)))
        else:
            off = ex[2] // tn
            in_specs.append(pl.BlockSpec((tm, tn), lambda i, j, k, off=off: (i, j + off)))
        args.append(ex[1])
    return pl.pallas_call(
        functools.partial(_mm_kernel, nk=nk, n_extra=len(extras), epilogue=epilogue),
        grid=(m // tm, n // tn, nk),
        in_specs=in_specs,
        out_specs=pl.BlockSpec((tm, tn), lambda i, j, k: (i, j)),
        out_shape=jax.ShapeDtypeStruct((m, n), out_dtype),
        scratch_shapes=[pltpu.VMEM((tm, tn), F32)] if nk > 1 else [],
        compiler_params=_params("parallel", "parallel", "arbitrary"),
        name=name,
    )(*args)


def _mix_kernel(a1_ref, w1_ref, a2_ref, w2_ref, g1_ref, g2_ref, o_ref):
    y1 = jnp.dot(a1_ref[...], w1_ref[...], preferred_element_type=F32)
    y2 = jnp.dot(a2_ref[...], w2_ref[...], preferred_element_type=F32)
    o_ref[...] = (g1_ref[...].astype(F32) * y1 + g2_ref[...].astype(F32) * y2).astype(o_ref.dtype)


def _gated_mix(a1, w1, a2, w2, gates, out_dtype, tm=1024, tn=1024):
    m, k1 = a1.shape
    k2 = a2.shape[1]
    n = w1.shape[1]
    tm, tn = min(tm, m), min(tn, n)
    nb = n // tn
    return pl.pallas_call(
        _mix_kernel,
        grid=(m // tm, nb),
        in_specs=[
            pl.BlockSpec((tm, k1), lambda i, j: (i, 0)),
            pl.BlockSpec((k1, tn), lambda i, j: (0, j)),
            pl.BlockSpec((tm, k2), lambda i, j: (i, 0)),
            pl.BlockSpec((k2, tn), lambda i, j: (0, j)),
            pl.BlockSpec((tm, tn), lambda i, j: (i, j)),
            pl.BlockSpec((tm, tn), lambda i, j: (i, j + nb)),
        ],
        out_specs=pl.BlockSpec((tm, tn), lambda i, j: (i, j)),
        out_shape=jax.ShapeDtypeStruct((m, n), out_dtype),
        compiler_params=_params("parallel", "parallel"),
        name="gated_mix",
    )(a1, w1, a2, w2, gates, gates)


def _conv_kernel(a_ref, g_ref, ah_ref, gh_ref, w_ref, cb_ref, lng_ref, lnb_ref, o_ref, hbuf, cbuf,
                 *, T, C, KW, HALO, RC):
    i = pl.program_id(1)
    hh = ah_ref[...] * jax.nn.sigmoid(gh_ref[...])
    hbuf[0:HALO, :] = jnp.where(i > 0, hh, 0.0)
    hbuf[HALO:HALO + T, :] = a_ref[...] * jax.nn.sigmoid(g_ref[...])
    base = HALO - (KW - 1)

    def chunk(c, carry):
        cs = pl.ds(pl.multiple_of(c * LANES, LANES), LANES)
        for r in range(T // RC):
            acc = jnp.zeros((RC, LANES), F32)
            for j in range(KW):
                acc = acc + w_ref[j:j + 1, cs] * hbuf[base + r * RC + j:base + r * RC + j + RC, cs]
            cbuf[r * RC:(r + 1) * RC, cs] = acc + cb_ref[:, cs]
        return carry

    lax.fori_loop(0, C // LANES, chunk, 0)

    RN = 16

    def norm(r, carry):
        rs = pl.ds(pl.multiple_of(r * RN, RN), RN)
        x = cbuf[rs, :]
        mu = jnp.mean(x, axis=-1, keepdims=True)
        xc = x - mu
        var = jnp.mean(xc * xc, axis=-1, keepdims=True)
        y = xc * lax.rsqrt(var + EPS) * lng_ref[...] + lnb_ref[...]
        o_ref[rs, :] = (y * jax.nn.sigmoid(y)).astype(o_ref.dtype)
        return carry

    lax.fori_loop(0, T // RN, norm, 0)


def _conv_branch(p_conv, conv_w, conv_bias, ln_g, ln_b, batch, seq, T=256):
    n, c2 = p_conv.shape
    C = c2 // 2
    KW = conv_w.shape[0]
    HALO = 32
    assert KW - 1 <= HALO
    T = min(T, seq)
    RC = min(64, T)
    nt = seq // T
    hb = T // HALO
    row = lambda b, i: b * nt + i
    halo_row = lambda b, i: jnp.maximum((b * nt + i) * hb - 1, 0)
    vec = lambda v: v.reshape(1, C)
    return pl.pallas_call(
        functools.partial(_conv_kernel, T=T, C=C, KW=KW, HALO=HALO, RC=RC),
        grid=(batch, nt),
        in_specs=[
            pl.BlockSpec((T, C), lambda b, i: (row(b, i), 0)),
            pl.BlockSpec((T, C), lambda b, i: (row(b, i), 1)),
            pl.BlockSpec((HALO, C), lambda b, i: (halo_row(b, i), 0)),
            pl.BlockSpec((HALO, C), lambda b, i: (halo_row(b, i), 1)),
            pl.BlockSpec((KW, C), lambda b, i: (0, 0)),
            pl.BlockSpec((1, C), lambda b, i: (0, 0)),
            pl.BlockSpec((1, C), lambda b, i: (0, 0)),
            pl.BlockSpec((1, C), lambda b, i: (0, 0)),
        ],
        out_specs=pl.BlockSpec((T, C), lambda b, i: (row(b, i), 0)),
        out_shape=jax.ShapeDtypeStruct((n, C), BF16),
        scratch_shapes=[pltpu.VMEM((HALO + T, C), F32), pltpu.VMEM((T, C), F32)],
        compiler_params=_params("parallel", "parallel"),
        name="conv_branch",
    )(p_conv, p_conv, p_conv, p_conv, conv_w, vec(conv_bias), vec(ln_g), vec(ln_b))


def _index_mask_kernel(q_ref, kt_ref, w_ref, o_ref, sc, wb, rhs, *, TQ, TS, S, G, HG, DI, topk, idx_bits):
    i = pl.program_id(1)
    nchunks = (i + 1) * (TQ // TS)
    NH = G * HG
    int_min = jnp.int32(-2 ** 31)

    for h in range(NH):
        wb[h] = jnp.broadcast_to(w_ref[:, h:h + 1], (TQ, TS))
    rhs[...] = jnp.zeros_like(rhs)
    q = q_ref[...].reshape(G * TQ, HG * DI)
    t_pos = i * TQ + lax.broadcasted_iota(jnp.int32, (TQ, TS), 0)
    lane = lax.broadcasted_iota(jnp.int32, (TQ, TS), 1)

    def score_chunk(c, carry):
        cs = pl.ds(pl.multiple_of(c * TS, TS), TS)
        kt = kt_ref[0, :, cs]
        for h in range(HG):
            rhs[h * DI:(h + 1) * DI, h * TS:(h + 1) * TS] = kt
        x = jnp.dot(q, rhs[...], preferred_element_type=F32)
        acc = jnp.zeros((TQ, TS), F32)
        for g in range(G):
            for h in range(HG):
                acc = acc + wb[g * HG + h] * jnp.maximum(x[g * TQ:(g + 1) * TQ, h * TS:(h + 1) * TS], 0.0)
        acc = acc + 0.0
        bits = pltpu.bitcast(acc, jnp.int32)
        key = jnp.where(bits < 0, bits ^ jnp.int32(0x7FFFFFFF), bits)
        s_pos = c * TS + lane
        sc[:, cs] = jnp.where(s_pos <= t_pos, key, int_min)
        return carry

    lax.fori_loop(0, nchunks, score_chunk, 0)

    def count(pred):
        def body(c, cnt):
            cs = pl.ds(pl.multiple_of(c * TS, TS), TS)
            return cnt + pred(sc[:, cs], c).astype(jnp.int32)

        cnt = lax.fori_loop(0, nchunks, body, jnp.zeros((TQ, TS), jnp.int32))
        return jnp.sum(cnt, axis=1, keepdims=True)

    def bit_step(b, thr):
        cand = thr + jnp.left_shift(jnp.int32(1), 31 - b)
        n_ge = count(lambda k, c: k >= cand)
        return jnp.where(n_ge >= topk, cand, thr)

    thr = lax.fori_loop(0, 32, bit_step, jnp.full((TQ, 1), int_min, jnp.int32))
    n_ge = count(lambda k, c: k >= thr)
    has_ties = jnp.max(n_ge) > topk

    def write(sel):
        def body(c, carry):
            cs = pl.ds(pl.multiple_of(c * TS, TS), TS)
            s_pos = c * TS + lane
            keep = sel(sc[:, cs], s_pos) & (s_pos <= t_pos)
            o_ref[0, :, cs] = jnp.where(keep, 0.0, MASKED).astype(o_ref.dtype)
            return carry

        lax.fori_loop(0, nchunks, body, 0)

    @pl.when(jnp.logical_not(has_ties))
    def _():
        write(lambda k, s_pos: k >= thr)

    @pl.when(has_ties)
    def _():
        need = topk - count(lambda k, c: k > thr)

        def idx_step(b, lim):
            cand = lim + jnp.left_shift(jnp.int32(1), idx_bits - 1 - b)
            n_before = count(lambda k, c: (k == thr) & ((c * TS + lane) < cand))
            return jnp.where(n_before < need, cand, lim)

        lim = lax.fori_loop(0, idx_bits, idx_step, jnp.zeros((TQ, 1), jnp.int32))
        write(lambda k, s_pos: (k > thr) | ((k == thr) & (s_pos <= lim)))

    def clear(c, carry):
        cs = pl.ds(pl.multiple_of(c * TS, TS), TS)
        o_ref[0, :, cs] = jnp.full((TQ, TS), MASKED, o_ref.dtype)
        return carry

    lax.fori_loop(nchunks, S // TS, clear, 0)


def _index_mask(qi_g, ki_t, wi, batch, seq, topk, TQ=256, TS=128):
    G, n, gw = qi_g.shape
    DI = ki_t.shape[1]
    HG = gw // DI
    NH = G * HG
    TQ = min(TQ, seq)
    TS = min(TS, TQ)
    nq = seq // TQ
    idx_bits = max(1, int(seq - 1).bit_length())
    return pl.pallas_call(
        functools.partial(_index_mask_kernel, TQ=TQ, TS=TS, S=seq, G=G, HG=HG, DI=DI, topk=topk, idx_bits=idx_bits),
        grid=(batch, nq),
        in_specs=[
            pl.BlockSpec((G, TQ, gw), lambda b, i: (0, b * nq + i, 0)),
            pl.BlockSpec((1, DI, seq), lambda b, i: (b, 0, 0)),
            pl.BlockSpec((TQ, NH), lambda b, i: (b * nq + i, 0)),
        ],
        out_specs=pl.BlockSpec((1, TQ, seq), lambda b, i: (b, i, 0)),
        out_shape=jax.ShapeDtypeStruct((batch, seq, seq), BF16),
        scratch_shapes=[
            pltpu.VMEM((TQ, seq), jnp.int32),
            pltpu.VMEM((NH, TQ, TS), F32),
            pltpu.VMEM((HG * DI, HG * TS), BF16),
        ],
        compiler_params=_params("parallel", "parallel"),
        name="index_mask",
    )(qi_g, ki_t, wi)


def _attn_kernel(q_ref, kt_ref, v_ref, mask_ref, bias_ref, o_ref, m_scr, l_scr, acc_scr, *, H, DH, NEAR, scale):
    i = pl.program_id(1)
    j = pl.program_id(2)

    @pl.when(j == 0)
    def _():
        m_scr[...] = jnp.full_like(m_scr, MASKED)
        l_scr[...] = jnp.zeros_like(l_scr)
        acc_scr[...] = jnp.zeros_like(acc_scr)

    @pl.when(j <= i)
    def _():
        mask = mask_ref[0].astype(F32)
        d = jnp.minimum(i - j, NEAR)
        for h in range(H):
            hs = slice(h * DH, (h + 1) * DH)
            s = jnp.dot(q_ref[:, hs], kt_ref[hs, :], preferred_element_type=F32)
            s = s * scale + bias_ref[h, d] + mask
            m_old = m_scr[h]
            m_new = jnp.maximum(m_old, jnp.max(s, axis=-1, keepdims=True))
            alpha = jnp.exp(m_old - m_new)
            p = jnp.exp(s - m_new)
            l_scr[h] = alpha * l_scr[h] + jnp.sum(p, axis=-1, keepdims=True)
            pv = jnp.dot(p.astype(v_ref.dtype), v_ref[:, hs], preferred_element_type=F32)
            acc_scr[:, hs] = alpha * acc_scr[:, hs] + pv
            m_scr[h] = m_new

    @pl.when(j == i)
    def _():
        for h in range(H):
            hs = slice(h * DH, (h + 1) * DH)
            o_ref[:, hs] = (acc_scr[:, hs] / l_scr[h]).astype(o_ref.dtype)


def _attention(q, k_t, v, mask, bias_tiles, batch, seq, n_heads, T):
    n, width = q.shape
    DH = width // n_heads
    nt = seq // T
    NEAR = bias_tiles.shape[1] - 1
    kj = lambda i, j: jnp.minimum(j, i)
    return pl.pallas_call(
        functools.partial(_attn_kernel, H=n_heads, DH=DH, NEAR=NEAR, scale=DH ** -0.5),
        grid=(batch, nt, nt),
        in_specs=[
            pl.BlockSpec((T, width), lambda b, i, j: (b * nt + i, 0)),
            pl.BlockSpec((width, T), lambda b, i, j: (0, b * nt + kj(i, j))),
            pl.BlockSpec((T, width), lambda b, i, j: (b * nt + kj(i, j), 0)),
            pl.BlockSpec((1, T, T), lambda b, i, j: (b, i, kj(i, j))),
            pl.BlockSpec(bias_tiles.shape, lambda b, i, j: (0, 0, 0, 0)),
        ],
        out_specs=pl.BlockSpec((T, width), lambda b, i, j: (b * nt + i, 0)),
        out_shape=jax.ShapeDtypeStruct((n, width), BF16),
        scratch_shapes=[
            pltpu.VMEM((n_heads, T, 1), F32),
            pltpu.VMEM((n_heads, T, 1), F32),
            pltpu.VMEM((T, width), F32),
        ],
        compiler_params=_params("parallel", "parallel", "arbitrary"),
        name="masked_attention",
    )(q, k_t, v, mask, bias_tiles)


def _t5_bucket(n):
    max_exact = N_BUCKETS // 2
    nf = jnp.maximum(n, 1).astype(F32)
    large = max_exact + (jnp.log(nf / max_exact) / math.log(MAX_DISTANCE / max_exact)
                         * (N_BUCKETS - max_exact)).astype(jnp.int32)
    large = jnp.minimum(large, N_BUCKETS - 1)
    return jnp.where(n < max_exact, n, large)


def _bias_tiles(rel_bias, T, near):
    t = jnp.arange(T, dtype=jnp.int32)
    tiles = []
    for d in range(near):
        dist = d * T + t[:, None] - t[None, :]
        tiles.append(rel_bias[_t5_bucket(jnp.maximum(dist, 0))])
    tiles.append(jnp.broadcast_to(rel_bias[N_BUCKETS - 1], (T, T, rel_bias.shape[1])))
    return jnp.transpose(jnp.stack(tiles), (3, 0, 1, 2)).astype(F32)


def kernel(x, norm1_g, w_in, b_gate, conv_w, conv_bias, conv_ln_g, conv_ln_b, w_conv_out, w_attn_out,
           rel_bias, w_o, norm2_g, w_ff1, w_ff2, normf_g):
    B, S, D = x.shape
    N = B * S
    depth = w_in.shape[0]
    C = conv_w.shape[-1]
    A = w_attn_out.shape[1]
    H = rel_bias.shape[1]
    NH, DI = IDX_HEADS, IDX_DIM
    HG = MXU_DIM // DI
    G = NH // HG
    topk = min(TOPK_MAX, S // 4)
    off_q = 2 * C
    off_k, off_v, off_iq = off_q + A, off_q + 2 * A, off_q + 3 * A
    off_ik = off_iq + NH * DI
    off_iw = off_ik + DI
    off_gate = off_iw + NH
    T_ATT = min(256, S)
    near = -(-(MAX_DISTANCE + T_ATT - 1) // T_ATT)
    bias_tiles = _bias_tiles(rel_bias, T_ATT, near)
    sigmoid_bias = lambda acc, b: jax.nn.sigmoid(acc + b)
    relu2 = lambda acc: jnp.square(jnp.maximum(acc, 0.0))
    add = lambda acc, r: acc + r

    h = x.reshape(N, D)
    for l in range(depth):
        wl = w_in[l]
        u = _rmsnorm(h, norm1_g[l], BF16)
        p_conv = _matmul(u, wl[:, :off_q].astype(BF16), F32, name="proj_conv")
        qkv = _matmul(u, wl[:, off_q:off_iq].astype(BF16), BF16, name="proj_qkv")
        qi = _matmul(u, wl[:, off_iq:off_ik].astype(BF16), BF16, name="proj_qi")
        w_small = jnp.pad(wl[:, off_ik:off_gate], ((0, 0), (0, LANES - (DI + NH)))).astype(BF16)
        kw = _matmul(u, w_small, F32, name="proj_ki_wi")
        gates = _matmul(u, wl[:, off_gate:].astype(BF16), BF16, epilogue=sigmoid_bias,
                        extras=[("row", b_gate[l].reshape(1, -1))], name="proj_gates")

        cf = _conv_branch(p_conv, conv_w[l], conv_bias[l], conv_ln_g[l], conv_ln_b[l], B, S)

        qi_g = jnp.transpose(qi.reshape(N, G, HG * DI), (1, 0, 2))
        ki_t = jnp.transpose(kw[:, :DI].astype(BF16).reshape(B, S, DI), (0, 2, 1))
        wi = kw[:, DI:DI + NH] * (NH ** -0.5) * (DI ** -0.5)
        mask = _index_mask(qi_g, ki_t, wi, B, S, topk)
        q = qkv[:, :A]
        k_t = jnp.transpose(qkv[:, A:2 * A])
        v = qkv[:, 2 * A:]
        o = _attention(q, k_t, v, mask, bias_tiles, B, S, H, T_ATT)

        mixed = _gated_mix(cf, w_conv_out[l].astype(BF16), o, w_attn_out[l].astype(BF16), gates, BF16)
        h = _matmul(mixed, w_o[l].astype(BF16), F32, epilogue=add, extras=[("tile", h, 0)], tn=512, name="w_o")
        u2 = _rmsnorm(h, norm2_g[l], BF16)
        act = _matmul(u2, w_ff1[l].astype(BF16), BF16, epilogue=relu2, name="ff1")
        h = _matmul(act, w_ff2[l].astype(BF16), F32, epilogue=add, extras=[("tile", h, 0)], tn=512, name="ff2")
    return _rmsnorm(h, normf_g, F32).reshape(B, S, D)
```

```python
import functools
import math

import jax
import jax.numpy as jnp
from jax import lax
from jax.experimental import pallas as pl
from jax.experimental.pallas import tpu as pltpu

IDX_HEADS = 32
IDX_DIM = 64
TOPK_MAX = 256
N_BUCKETS = 32
MAX_DISTANCE = 128
EPS = 1e-6

V7X_VMEM_BYTES = 64 * 1024 * 1024
VMEM_LIMIT_BYTES = V7X_VMEM_BYTES - 8 * 1024 * 1024
LANES = 128
SUBLANES = 8
BF16_ROWS = 16
MXU_DIM = 256
MASKED = -1e30

F32 = jnp.float32
BF16 = jnp.bfloat16


def _params(*sem):
    return pltpu.CompilerParams(dimension_semantics=sem, vmem_limit_bytes=VMEM_LIMIT_BYTES)


def _rmsnorm_kernel(x_ref, g_ref, o_ref):
    x = x_ref[...]
    ms = jnp.mean(x * x, axis=-1, keepdims=True)
    o_ref[...] = (x * lax.rsqrt(ms + EPS) * g_ref[...]).astype(o_ref.dtype)


def _rmsnorm(x, g, out_dtype, tr=256):
    n, d = x.shape
    tr = min(tr, n)
    return pl.pallas_call(
        _rmsnorm_kernel,
        grid=(n // tr,),
        in_specs=[pl.BlockSpec((tr, d), lambda i: (i, 0)), pl.BlockSpec((1, d), lambda i: (0, 0))],
        out_specs=pl.BlockSpec((tr, d), lambda i: (i, 0)),
        out_shape=jax.ShapeDtypeStruct((n, d), out_dtype),
        compiler_params=_params("parallel"),
        name="rmsnorm",
    )(x, g.reshape(1, d))


def _mm_kernel(*refs, nk, n_extra, epilogue, group_major):
    a_ref, w_ref = refs[0], refs[1]
    extra = refs[2:2 + n_extra]
    o_ref = refs[2 + n_extra]

    def emit(acc):
        res = epilogue(acc, *[e[...] for e in extra]).astype(o_ref.dtype)
        if group_major:
            o_ref[0] = res
        else:
            o_ref[...] = res

    if nk == 1:
        emit(jnp.dot(a_ref[...], w_ref[...], preferred_element_type=F32))
        return
    acc_ref = refs[3 + n_extra]
    k = pl.program_id(2)

    @pl.when(k == 0)
    def _():
        acc_ref[...] = jnp.zeros_like(acc_ref)

    acc_ref[...] += jnp.dot(a_ref[...], w_ref[...], preferred_element_type=F32)

    @pl.when(k == nk - 1)
    def _():
        emit(acc_ref[...])


def _matmul(a, w, out_dtype, *, epilogue=None, extras=(), tm=1024, tn=1024, tk=4096, group_major=False,
            name="matmul"):
    m, kd = a.shape
    n = w.shape[1]
    tm, tn, tk = min(tm, m), min(tn, n), min(tk, kd)
    nk = kd // tk
    if epilogue is None:
        epilogue = lambda acc: acc
    in_specs = [pl.BlockSpec((tm, tk), lambda i, j, k: (i, k)), pl.BlockSpec((tk, tn), lambda i, j, k: (k, j))]
    args = [a, w]
    for ex in extras:
        if ex[0] == "row":
            in_specs.append(pl.BlockSpec((1, tn), lambda i, j, k: (0, j)))
        else:
            off = ex[2] // tn
            in_specs.append(pl.BlockSpec((tm, tn), lambda i, j, k, off=off: (i, j + off)))
        args.append(ex[1])
    if group_major:
        out_specs = pl.BlockSpec((1, tm, tn), lambda i, j, k: (j, i, 0))
        out_shape = jax.ShapeDtypeStruct((n // tn, m, tn), out_dtype)
    else:
        out_specs = pl.BlockSpec((tm, tn), lambda i, j, k: (i, j))
        out_shape = jax.ShapeDtypeStruct((m, n), out_dtype)
    return pl.pallas_call(
        functools.partial(_mm_kernel, nk=nk, n_extra=len(extras), epilogue=epilogue, group_major=group_major),
        grid=(m // tm, n // tn, nk),
        in_specs=in_specs,
        out_specs=out_specs,
        out_shape=out_shape,
        scratch_shapes=[pltpu.VMEM((tm, tn), F32)] if nk > 1 else [],
        compiler_params=_params("parallel", "parallel", "arbitrary"),
        name=name,
    )(*args)


def _mix_kernel(a1_ref, w1_ref, a2_ref, w2_ref, g1_ref, g2_ref, o_ref):
    y1 = jnp.dot(a1_ref[...], w1_ref[...], preferred_element_type=F32)
    y2 = jnp.dot(a2_ref[...], w2_ref[...], preferred_element_type=F32)
    o_ref[...] = (g1_ref[...].astype(F32) * y1 + g2_ref[...].astype(F32) * y2).astype(o_ref.dtype)


def _gated_mix(a1, w1, a2, w2, gates, out_dtype, tm=1024, tn=1024):
    m, k1 = a1.shape
    k2 = a2.shape[1]
    n = w1.shape[1]
    tm, tn = min(tm, m), min(tn, n)
    nb = n // tn
    return pl.pallas_call(
        _mix_kernel,
        grid=(m // tm, nb),
        in_specs=[
            pl.BlockSpec((tm, k1), lambda i, j: (i, 0)),
            pl.BlockSpec((k1, tn), lambda i, j: (0, j)),
            pl.BlockSpec((tm, k2), lambda i, j: (i, 0)),
            pl.BlockSpec((k2, tn), lambda i, j: (0, j)),
            pl.BlockSpec((tm, tn), lambda i, j: (i, j)),
            pl.BlockSpec((tm, tn), lambda i, j: (i, j + nb)),
        ],
        out_specs=pl.BlockSpec((tm, tn), lambda i, j: (i, j)),
        out_shape=jax.ShapeDtypeStruct((m, n), out_dtype),
        compiler_params=_params("parallel", "parallel"),
        name="gated_mix",
    )(a1, w1, a2, w2, gates, gates)


def _conv_kernel(a_ref, g_ref, ah_ref, gh_ref, w_ref, cb_ref, lng_ref, lnb_ref, o_ref, hbuf, cbuf,
                 *, T, C, KW, HALO, RC):
    i = pl.program_id(1)
    hh = ah_ref[...] * jax.nn.sigmoid(gh_ref[...])
    hbuf[0:HALO, :] = jnp.where(i > 0, hh, 0.0)
    hbuf[HALO:HALO + T, :] = a_ref[...] * jax.nn.sigmoid(g_ref[...])
    base = HALO - (KW - 1)

    def chunk(c, carry):
        cs = pl.ds(pl.multiple_of(c * LANES, LANES), LANES)
        for r in range(T // RC):
            acc = jnp.zeros((RC, LANES), F32)
            for j in range(KW):
                acc = acc + w_ref[j:j + 1, cs] * hbuf[base + r * RC + j:base + r * RC + j + RC, cs]
            cbuf[r * RC:(r + 1) * RC, cs] = acc + cb_ref[:, cs]
        return carry

    lax.fori_loop(0, C // LANES, chunk, 0)

    RN = 16

    def norm(r, carry):
        rs = pl.ds(pl.multiple_of(r * RN, RN), RN)
        x = cbuf[rs, :]
        mu = jnp.mean(x, axis=-1, keepdims=True)
        xc = x - mu
        var = jnp.mean(xc * xc, axis=-1, keepdims=True)
        y = xc * lax.rsqrt(var + EPS) * lng_ref[...] + lnb_ref[...]
        o_ref[rs, :] = (y * jax.nn.sigmoid(y)).astype(o_ref.dtype)
        return carry

    lax.fori_loop(0, T // RN, norm, 0)


def _conv_branch(p_conv, conv_w, conv_bias, ln_g, ln_b, batch, seq, T=256):
    n, c2 = p_conv.shape
    C = c2 // 2
    KW = conv_w.shape[0]
    HALO = 32
    assert KW - 1 <= HALO
    T = min(T, seq)
    RC = min(64, T)
    nt = seq // T
    hb = T // HALO
    row = lambda b, i: b * nt + i
    halo_row = lambda b, i: jnp.maximum((b * nt + i) * hb - 1, 0)
    vec = lambda v: v.reshape(1, C)
    return pl.pallas_call(
        functools.partial(_conv_kernel, T=T, C=C, KW=KW, HALO=HALO, RC=RC),
        grid=(batch, nt),
        in_specs=[
            pl.BlockSpec((T, C), lambda b, i: (row(b, i), 0)),
            pl.BlockSpec((T, C), lambda b, i: (row(b, i), 1)),
            pl.BlockSpec((HALO, C), lambda b, i: (halo_row(b, i), 0)),
            pl.BlockSpec((HALO, C), lambda b, i: (halo_row(b, i), 1)),
            pl.BlockSpec((KW, C), lambda b, i: (0, 0)),
            pl.BlockSpec((1, C), lambda b, i: (0, 0)),
            pl.BlockSpec((1, C), lambda b, i: (0, 0)),
            pl.BlockSpec((1, C), lambda b, i: (0, 0)),
        ],
        out_specs=pl.BlockSpec((T, C), lambda b, i: (row(b, i), 0)),
        out_shape=jax.ShapeDtypeStruct((n, C), BF16),
        scratch_shapes=[pltpu.VMEM((HALO + T, C), F32), pltpu.VMEM((T, C), F32)],
        compiler_params=_params("parallel", "parallel"),
        name="conv_branch",
    )(p_conv, p_conv, p_conv, p_conv, conv_w, vec(conv_bias), vec(ln_g), vec(ln_b))


def _index_mask_kernel(q_ref, kt_ref, w_ref, o_ref, sc, wb, rhs, *, TQ, TS, S, G, HG, DI, topk, idx_bits):
    i = pl.program_id(1)
    nchunks = (i + 1) * (TQ // TS)
    NH = G * HG
    CR = TQ
    NACC = 4
    int_min = jnp.int32(-2 ** 31)

    for h in range(NH):
        wb[h] = jnp.broadcast_to(w_ref[:, h:h + 1], (TQ, TS))
    rhs[...] = jnp.zeros_like(rhs)
    q = q_ref[...].reshape(G * TQ, HG * DI)
    t_pos_c = i * TQ + lax.broadcasted_iota(jnp.int32, (TS, TQ), 1)
    row_c = lax.broadcasted_iota(jnp.int32, (TS, TQ), 0)

    def score_chunk(c, carry):
        cs = pl.ds(pl.multiple_of(c * TS, TS), TS)
        kt = kt_ref[0, :, cs]
        for h in range(HG):
            rhs[h * DI:(h + 1) * DI, h * TS:(h + 1) * TS] = kt
        x = jnp.dot(q, rhs[...], preferred_element_type=F32)
        acc = jnp.zeros((TQ, TS), F32)
        for g in range(G):
            for h in range(HG):
                acc = acc + wb[g * HG + h] * jnp.maximum(x[g * TQ:(g + 1) * TQ, h * TS:(h + 1) * TS], 0.0)
        acc = acc.T + 0.0
        bits = pltpu.bitcast(acc, jnp.int32)
        key = jnp.where(bits < 0, bits ^ jnp.int32(0x7FFFFFFF), bits)
        sc[cs, :] = jnp.where(c * TS + row_c <= t_pos_c, key, int_min)
        return carry

    lax.fori_loop(0, nchunks, score_chunk, 0)

    t_pos = i * TQ + lax.broadcasted_iota(jnp.int32, (CR, TQ), 1)
    row = lax.broadcasted_iota(jnp.int32, (CR, TQ), 0)
    row3 = row.reshape(CR // SUBLANES, SUBLANES, TQ)

    def count(pred):
        def body(c, cnt):
            k3 = sc[pl.ds(pl.multiple_of(c * CR, CR), CR), :].reshape(CR // SUBLANES, SUBLANES, TQ)
            hit = pred(k3, c * CR + row3).astype(jnp.int32)
            return cnt + jnp.sum(hit.reshape(CR // (SUBLANES * NACC), NACC, SUBLANES, TQ), axis=0)

        cnt = lax.fori_loop(0, i + 1, body, jnp.zeros((NACC, SUBLANES, TQ), jnp.int32))
        return jnp.sum(cnt.reshape(NACC * SUBLANES, TQ), axis=0, keepdims=True)

    bc = lambda v: jnp.broadcast_to(v, (SUBLANES, TQ))[None]

    def bit_step(b, thr):
        cand = thr + jnp.left_shift(jnp.int32(1), 31 - b)
        cand_b = bc(cand)
        n_ge = count(lambda k, s: k >= cand_b)
        return jnp.where(n_ge >= topk, cand, thr)

    thr = lax.fori_loop(0, 32, bit_step, jnp.full((1, TQ), int_min, jnp.int32))
    thr_b = bc(thr)
    n_ge = count(lambda k, s: k >= thr_b)
    has_ties = jnp.max(n_ge) > topk

    def write(sel):
        def body(c, carry):
            rs = pl.ds(pl.multiple_of(c * CR, CR), CR)
            s_pos = c * CR + row
            keep = sel(sc[rs, :], s_pos) & (s_pos <= t_pos)
            o_ref[0, rs, :] = jnp.where(keep, 0.0, MASKED).astype(o_ref.dtype)
            return carry

        lax.fori_loop(0, i + 1, body, 0)

    @pl.when(jnp.logical_not(has_ties))
    def _():
        write(lambda k, s_pos: k >= thr)

    @pl.when(has_ties)
    def _():
        need = bc(topk - count(lambda k, s: k > thr_b))

        def idx_step(b, lim):
            cand = lim + jnp.left_shift(jnp.int32(1), idx_bits - 1 - b)
            cand_b = bc(cand)
            n_before = count(lambda k, s: (k == thr_b) & (s < cand_b))
            return jnp.where(n_before < need[0, 0:1], cand, lim)

        lim = lax.fori_loop(0, idx_bits, idx_step, jnp.zeros((1, TQ), jnp.int32))
        write(lambda k, s_pos: (k > thr) | ((k == thr) & (s_pos <= lim)))

    def clear(c, carry):
        o_ref[0, pl.ds(pl.multiple_of(c * CR, CR), CR), :] = jnp.full((CR, TQ), MASKED, o_ref.dtype)
        return carry

    lax.fori_loop(i + 1, S // CR, clear, 0)


def _index_mask(qi_g, ki_t, wi, batch, seq, topk, TQ=256, TS=128):
    G, n, gw = qi_g.shape
    DI = ki_t.shape[1]
    HG = gw // DI
    NH = G * HG
    TQ = min(TQ, seq)
    TS = min(TS, TQ)
    nq = seq // TQ
    idx_bits = max(1, int(seq - 1).bit_length())
    return pl.pallas_call(
        functools.partial(_index_mask_kernel, TQ=TQ, TS=TS, S=seq, G=G, HG=HG, DI=DI, topk=topk, idx_bits=idx_bits),
        grid=(batch, nq),
        in_specs=[
            pl.BlockSpec((G, TQ, gw), lambda b, i: (0, b * nq + i, 0)),
            pl.BlockSpec((1, DI, seq), lambda b, i: (b, 0, 0)),
            pl.BlockSpec((TQ, NH), lambda b, i: (b * nq + i, 0)),
        ],
        out_specs=pl.BlockSpec((1, seq, TQ), lambda b, i: (b, 0, i)),
        out_shape=jax.ShapeDtypeStruct((batch, seq, seq), BF16),
        scratch_shapes=[
            pltpu.VMEM((seq, TQ), jnp.int32),
            pltpu.VMEM((NH, TQ, TS), F32),
            pltpu.VMEM((HG * DI, HG * TS), BF16),
        ],
        compiler_params=_params("parallel", "parallel"),
        name="index_mask",
    )(qi_g, ki_t, wi)


def _attn_kernel(ii_ref, jj_ref, qt_ref, k_ref, vt_ref, mask_ref, bias_ref, bfar_ref, o_ref,
                 m_scr, r_scr, a_scr, z_scr, acc_scr, *, H, DH, VA, NEAR, c2):
    pair = pl.program_id(1)
    i = ii_ref[pair]
    j = jj_ref[pair]

    @pl.when(j == 0)
    def _():
        m_scr[...] = jnp.full_like(m_scr, MASKED)
        acc_scr[...] = jnp.zeros_like(acc_scr)

    def scores(near):
        mask = mask_ref[0].astype(F32)
        for h in range(H):
            hs = slice(h * DH, (h + 1) * DH)
            z = jnp.dot(k_ref[:, hs], qt_ref[hs, :], preferred_element_type=F32) + mask
            if near is not None:
                z = z + bias_ref[h, near]
                shift = 0.0
            else:
                shift = bfar_ref[h]
            z_scr[h] = z
            m_old = m_scr[h]
            m_new = jnp.maximum(m_old, jnp.max(z, axis=0, keepdims=True) + shift)
            a_scr[h] = jnp.exp2((m_old - m_new) * c2)
            r_scr[h] = m_new - shift
            m_scr[h] = m_new

    for d in range(NEAR):
        pl.when(i - j == d)(functools.partial(scores, d))
    pl.when(i - j >= NEAR)(functools.partial(scores, None))

    for h in range(H):
        p = jnp.exp2((z_scr[h] - r_scr[h]) * c2)
        pv = jnp.dot(vt_ref[h * VA:(h + 1) * VA, :], p.astype(vt_ref.dtype), preferred_element_type=F32)
        acc_scr[h] = a_scr[h] * acc_scr[h] + pv

    @pl.when(j == i)
    def _():
        for h in range(H):
            a = acc_scr[h]
            o_ref[:, h * DH:(h + 1) * DH] = (a[:DH] / a[DH:DH + 1]).T.astype(o_ref.dtype)


def _attention(q_t, qkv, k_col, vt_aug, mask_t, bias_t, bias_far, batch, seq, n_heads, T):
    width, n = q_t.shape
    DH = width // n_heads
    VA = vt_aug.shape[0] // n_heads
    nt = seq // T
    NEAR = bias_t.shape[1]
    c2 = (DH ** -0.5) * math.log2(math.e)
    pairs = [(i, j) for i in range(nt) for j in range(i + 1)]
    ii = jnp.asarray([p[0] for p in pairs], jnp.int32)
    jj = jnp.asarray([p[1] for p in pairs], jnp.int32)
    stat = pltpu.VMEM((n_heads, 1, T), F32)
    grid_spec = pltpu.PrefetchScalarGridSpec(
        num_scalar_prefetch=2,
        grid=(batch, len(pairs)),
        in_specs=[
            pl.BlockSpec((width, T), lambda b, p, ii, jj: (0, b * nt + ii[p])),
            pl.BlockSpec((T, width), lambda b, p, ii, jj: (b * nt + jj[p], k_col)),
            pl.BlockSpec((n_heads * VA, T), lambda b, p, ii, jj: (0, b * nt + jj[p])),
            pl.BlockSpec((1, T, T), lambda b, p, ii, jj: (b, jj[p], ii[p])),
            pl.BlockSpec(bias_t.shape, lambda b, p, ii, jj: (0, 0, 0, 0)),
            pl.BlockSpec(bias_far.shape, lambda b, p, ii, jj: (0, 0, 0)),
        ],
        out_specs=pl.BlockSpec((T, width), lambda b, p, ii, jj: (b * nt + ii[p], 0)),
        scratch_shapes=[stat, stat, stat, pltpu.VMEM((n_heads, T, T), F32), pltpu.VMEM((n_heads, VA, T), F32)],
    )
    return pl.pallas_call(
        functools.partial(_attn_kernel, H=n_heads, DH=DH, VA=VA, NEAR=NEAR, c2=c2),
        grid_spec=grid_spec,
        out_shape=jax.ShapeDtypeStruct((n, width), BF16),
        compiler_params=_params("parallel", "arbitrary"),
        name="masked_attention",
    )(ii, jj, q_t, qkv, vt_aug, mask_t, bias_t, bias_far)


def _t5_bucket(n):
    max_exact = N_BUCKETS // 2
    nf = jnp.maximum(n, 1).astype(F32)
    large = max_exact + (jnp.log(nf / max_exact) / math.log(MAX_DISTANCE / max_exact)
                         * (N_BUCKETS - max_exact)).astype(jnp.int32)
    large = jnp.minimum(large, N_BUCKETS - 1)
    return jnp.where(n < max_exact, n, large)


def _bias_tiles(rel_bias, T, near, scale):
    s = jnp.arange(T, dtype=jnp.int32)[:, None]
    t = jnp.arange(T, dtype=jnp.int32)[None, :]
    rb = rel_bias.astype(F32) / scale
    tiles = []
    for d in range(near):
        bucket = _t5_bucket(jnp.maximum(d * T + t - s, 0))
        onehot = bucket[None, :, :] == jnp.arange(N_BUCKETS, dtype=jnp.int32)[:, None, None]
        tiles.append(jnp.sum(jnp.where(onehot[:, None], rb[:, :, None, None], 0.0), axis=0))
    far = jnp.broadcast_to(rb[N_BUCKETS - 1][:, None, None], (rb.shape[1], 1, T))
    return jnp.stack(tiles, axis=1), far


def kernel(x, norm1_g, w_in, b_gate, conv_w, conv_bias, conv_ln_g, conv_ln_b, w_conv_out, w_attn_out,
           rel_bias, w_o, norm2_g, w_ff1, w_ff2, normf_g):
    B, S, D = x.shape
    N = B * S
    depth = w_in.shape[0]
    C = conv_w.shape[-1]
    A = w_attn_out.shape[1]
    H = rel_bias.shape[1]
    DH = A // H
    NH, DI = IDX_HEADS, IDX_DIM
    HG = MXU_DIM // DI
    topk = min(TOPK_MAX, S // 4)
    off_q = 2 * C
    off_iq = off_q + 3 * A
    off_ik = off_iq + NH * DI
    off_iw = off_ik + DI
    off_gate = off_iw + NH
    T_ATT = min(256, S)
    near = -(-(MAX_DISTANCE + T_ATT - 1) // T_ATT)
    bias_t, bias_far = _bias_tiles(rel_bias, T_ATT, near, DH ** -0.5)
    sigmoid_bias = lambda acc, b: jax.nn.sigmoid(acc + b)
    relu2 = lambda acc: jnp.square(jnp.maximum(acc, 0.0))
    add = lambda acc, r: acc + r

    h = x.reshape(N, D)
    for l in range(depth):
        wl = w_in[l]
        u = _rmsnorm(h, norm1_g[l], BF16)
        p_conv = _matmul(u, wl[:, :off_q].astype(BF16), F32, name="proj_conv")
        qkv = _matmul(u, wl[:, off_q:off_iq].astype(BF16), BF16, name="proj_qkv")
        qi_g = _matmul(u, wl[:, off_iq:off_ik].astype(BF16), BF16, tn=HG * DI, group_major=True, name="proj_qi")
        w_small = jnp.pad(wl[:, off_ik:off_gate], ((0, 0), (0, LANES - (DI + NH)))).astype(BF16)
        kw = _matmul(u, w_small, F32, name="proj_ki_wi")
        gates = _matmul(u, wl[:, off_gate:].astype(BF16), BF16, epilogue=sigmoid_bias,
                        extras=[("row", b_gate[l].reshape(1, -1))], name="proj_gates")

        cf = _conv_branch(p_conv, conv_w[l], conv_bias[l], conv_ln_g[l], conv_ln_b[l], B, S)

        ki_t = jnp.transpose(kw[:, :DI].astype(BF16).reshape(B, S, DI), (0, 2, 1))
        wi = kw[:, DI:DI + NH] * (NH ** -0.5) * (DI ** -0.5)
        mask_t = _index_mask(qi_g, ki_t, wi, B, S, topk)
        q_t = jnp.transpose(qkv[:, :A])
        v_t = jnp.transpose(qkv[:, 2 * A:]).reshape(H, DH, N)
        vt_aug = jnp.concatenate([v_t, jnp.ones((H, BF16_ROWS, N), BF16)], axis=1).reshape(H * (DH + BF16_ROWS), N)
        o = _attention(q_t, qkv, 1, vt_aug, mask_t, bias_t, bias_far, B, S, H, T_ATT)

        mixed = _gated_mix(cf, w_conv_out[l].astype(BF16), o, w_attn_out[l].astype(BF16), gates, BF16)
        h = _matmul(mixed, w_o[l].astype(BF16), F32, epilogue=add, extras=[("tile", h, 0)], tn=512, name="w_o")
        u2 = _rmsnorm(h, norm2_g[l], BF16)
        act = _matmul(u2, w_ff1[l].astype(BF16), BF16, epilogue=relu2, name="ff1")
        h = _matmul(act, w_ff2[l].astype(BF16), F32, epilogue=add, extras=[("tile", h, 0)], tn=512, name="ff2")
    return _rmsnorm(h, normf_g, F32).reshape(B, S, D)
```

```python
import functools
import math

import jax
import jax.numpy as jnp
from jax import lax
from jax.experimental import pallas as pl
from jax.experimental.pallas import tpu as pltpu

IDX_HEADS = 32
IDX_DIM = 64
TOPK_MAX = 256
N_BUCKETS = 32
MAX_DISTANCE = 128
EPS = 1e-6

V7X_VMEM_BYTES = 64 * 1024 * 1024
VMEM_LIMIT_BYTES = V7X_VMEM_BYTES - 8 * 1024 * 1024
LANES = 128
SUBLANES = 8
BF16_ROWS = 16
MXU_DIM = 256
MASKED = -1e30

F32 = jnp.float32
BF16 = jnp.bfloat16


def _params(*sem):
    return pltpu.CompilerParams(dimension_semantics=sem, vmem_limit_bytes=VMEM_LIMIT_BYTES)


def _rmsnorm_kernel(x_ref, g_ref, o_ref):
    x = x_ref[...]
    ms = jnp.mean(x * x, axis=-1, keepdims=True)
    o_ref[...] = (x * lax.rsqrt(ms + EPS) * g_ref[...]).astype(o_ref.dtype)


def _rmsnorm(x, g, out_dtype, tr=256):
    n, d = x.shape
    tr = min(tr, n)
    return pl.pallas_call(
        _rmsnorm_kernel,
        grid=(n // tr,),
        in_specs=[pl.BlockSpec((tr, d), lambda i: (i, 0)), pl.BlockSpec((1, d), lambda i: (0, 0))],
        out_specs=pl.BlockSpec((tr, d), lambda i: (i, 0)),
        out_shape=jax.ShapeDtypeStruct((n, d), out_dtype),
        compiler_params=_params("parallel"),
        name="rmsnorm",
    )(x, g.reshape(1, d))


def _mm_kernel(*refs, nk, n_extra, epilogue, group_width):
    a_ref, w_ref = refs[0], refs[1]
    extra = refs[2:2 + n_extra]
    o_ref = refs[2 + n_extra]

    def emit(acc):
        res = epilogue(acc, *[e[...] for e in extra]).astype(o_ref.dtype)
        if group_width:
            for g in range(res.shape[1] // group_width):
                o_ref[g] = res[:, g * group_width:(g + 1) * group_width]
        else:
            o_ref[...] = res

    if nk == 1:
        emit(jnp.dot(a_ref[...], w_ref[...], preferred_element_type=F32))
        return
    acc_ref = refs[3 + n_extra]
    k = pl.program_id(2)

    @pl.when(k == 0)
    def _():
        acc_ref[...] = jnp.zeros_like(acc_ref)

    acc_ref[...] += jnp.dot(a_ref[...], w_ref[...], preferred_element_type=F32)

    @pl.when(k == nk - 1)
    def _():
        emit(acc_ref[...])


def _matmul(a, w, out_dtype, *, epilogue=None, extras=(), tm=1024, tn=1024, tk=4096, group_width=0,
            name="matmul"):
    m, kd = a.shape
    n = w.shape[1]
    tm, tn, tk = min(tm, m), min(tn, n), min(tk, kd)
    nk = kd // tk
    if epilogue is None:
        epilogue = lambda acc: acc
    in_specs = [pl.BlockSpec((tm, tk), lambda i, j, k: (i, k)), pl.BlockSpec((tk, tn), lambda i, j, k: (k, j))]
    args = [a, w]
    for ex in extras:
        if ex[0] == "row":
            in_specs.append(pl.BlockSpec((1, tn), lambda i, j, k: (0, j)))
        else:
            off = ex[2] // tn
            in_specs.append(pl.BlockSpec((tm, tn), lambda i, j, k, off=off: (i, j + off)))
        args.append(ex[1])
    if group_width:
        gpt = tn // group_width
        out_specs = pl.BlockSpec((gpt, tm, group_width), lambda i, j, k: (j, i, 0))
        out_shape = jax.ShapeDtypeStruct((n // group_width, m, group_width), out_dtype)
    else:
        out_specs = pl.BlockSpec((tm, tn), lambda i, j, k: (i, j))
        out_shape = jax.ShapeDtypeStruct((m, n), out_dtype)
    return pl.pallas_call(
        functools.partial(_mm_kernel, nk=nk, n_extra=len(extras), epilogue=epilogue, group_width=group_width),
        grid=(m // tm, n // tn, nk),
        in_specs=in_specs,
        out_specs=out_specs,
        out_shape=out_shape,
        scratch_shapes=[pltpu.VMEM((tm, tn), F32)] if nk > 1 else [],
        compiler_params=_params("parallel", "parallel", "arbitrary"),
        name=name,
    )(*args)


def _mm_t_kernel(a_ref, w_ref, o_ref, *, DH, ones_rows):
    res = jnp.dot(a_ref[...], w_ref[...], preferred_element_type=F32).T.astype(o_ref.dtype)
    if not ones_rows:
        o_ref[...] = res
        return
    VA = DH + ones_rows
    for h in range(res.shape[0] // DH):
        o_ref[h * VA:h * VA + DH, :] = res[h * DH:(h + 1) * DH, :]
        o_ref[h * VA + DH:(h + 1) * VA, :] = jnp.ones((ones_rows, res.shape[1]), o_ref.dtype)


def _matmul_t(a, w, out_dtype, *, tm=1024, tn=1024, DH=0, ones_rows=0, name="matmul_t"):
    m, kd = a.shape
    n = w.shape[1]
    tm, tn = min(tm, m), min(tn, n)
    rows = tn // DH * (DH + ones_rows) if ones_rows else tn
    return pl.pallas_call(
        functools.partial(_mm_t_kernel, DH=DH, ones_rows=ones_rows),
        grid=(m // tm, n // tn),
        in_specs=[pl.BlockSpec((tm, kd), lambda i, j: (i, 0)), pl.BlockSpec((kd, tn), lambda i, j: (0, j))],
        out_specs=pl.BlockSpec((rows, tm), lambda i, j: (j, i)),
        out_shape=jax.ShapeDtypeStruct((n // tn * rows, m), out_dtype),
        compiler_params=_params("parallel", "parallel"),
        name=name,
    )(a, w)


def _mix_kernel(a1_ref, w1_ref, a2_ref, w2_ref, g1_ref, g2_ref, o_ref):
    y1 = jnp.dot(a1_ref[...], w1_ref[...], preferred_element_type=F32)
    y2 = jnp.dot(a2_ref[...], w2_ref[...], preferred_element_type=F32)
    o_ref[...] = (g1_ref[...].astype(F32) * y1 + g2_ref[...].astype(F32) * y2).astype(o_ref.dtype)


def _gated_mix(a1, w1, a2, w2, gates, out_dtype, tm=1024, tn=1024):
    m, k1 = a1.shape
    k2 = a2.shape[1]
    n = w1.shape[1]
    tm, tn = min(tm, m), min(tn, n)
    nb = n // tn
    return pl.pallas_call(
        _mix_kernel,
        grid=(m // tm, nb),
        in_specs=[
            pl.BlockSpec((tm, k1), lambda i, j: (i, 0)),
            pl.BlockSpec((k1, tn), lambda i, j: (0, j)),
            pl.BlockSpec((tm, k2), lambda i, j: (i, 0)),
            pl.BlockSpec((k2, tn), lambda i, j: (0, j)),
            pl.BlockSpec((tm, tn), lambda i, j: (i, j)),
            pl.BlockSpec((tm, tn), lambda i, j: (i, j + nb)),
        ],
        out_specs=pl.BlockSpec((tm, tn), lambda i, j: (i, j)),
        out_shape=jax.ShapeDtypeStruct((m, n), out_dtype),
        compiler_params=_params("parallel", "parallel"),
        name="gated_mix",
    )(a1, w1, a2, w2, gates, gates)


def _conv_kernel(a_ref, g_ref, ah_ref, gh_ref, w_ref, cb_ref, lng_ref, lnb_ref, o_ref, hbuf, cbuf, sh,
                 *, T, C, KW, HALO, RC):
    i = pl.program_id(1)
    hh = ah_ref[...] * jax.nn.sigmoid(gh_ref[...])
    hbuf[0:HALO, :] = jnp.where(i > 0, hh, 0.0)
    hbuf[HALO:HALO + T, :] = a_ref[...] * jax.nn.sigmoid(g_ref[...])
    base = HALO - (KW - 1)
    NR = T + HALO - SUBLANES

    def chunk(c, carry):
        cs = pl.ds(pl.multiple_of(c * LANES, LANES), LANES)
        for b in range(1, SUBLANES):
            sh[b - 1, 0:NR, :] = hbuf[b:b + NR, cs]
        for r in range(T // RC):
            acc = jnp.zeros((RC, LANES), F32)
            for j in range(KW):
                a8, b = divmod(base + j, SUBLANES)
                r0 = r * RC + a8 * SUBLANES
                src = hbuf[r0:r0 + RC, cs] if b == 0 else sh[b - 1, r0:r0 + RC, :]
                acc = acc + w_ref[j:j + 1, cs] * src
            cbuf[r * RC:(r + 1) * RC, cs] = acc + cb_ref[:, cs]
        return carry

    lax.fori_loop(0, C // LANES, chunk, 0)

    RN = min(32, T)

    def norm(r, carry):
        rs = pl.ds(pl.multiple_of(r * RN, RN), RN)
        x = cbuf[rs, :]
        mu = jnp.mean(x, axis=-1, keepdims=True)
        xc = x - mu
        var = jnp.mean(xc * xc, axis=-1, keepdims=True)
        y = xc * lax.rsqrt(var + EPS) * lng_ref[...] + lnb_ref[...]
        o_ref[rs, :] = (y * jax.nn.sigmoid(y)).astype(o_ref.dtype)
        return carry

    lax.fori_loop(0, T // RN, norm, 0)


def _conv_branch(p_conv, conv_w, conv_bias, ln_g, ln_b, batch, seq, T=256):
    n, c2 = p_conv.shape
    C = c2 // 2
    KW = conv_w.shape[0]
    HALO = 32
    assert KW - 1 <= HALO
    T = min(T, seq)
    RC = min(64, T)
    nt = seq // T
    hb = T // HALO
    row = lambda b, i: b * nt + i
    halo_row = lambda b, i: jnp.maximum((b * nt + i) * hb - 1, 0)
    vec = lambda v: v.reshape(1, C)
    return pl.pallas_call(
        functools.partial(_conv_kernel, T=T, C=C, KW=KW, HALO=HALO, RC=RC),
        grid=(batch, nt),
        in_specs=[
            pl.BlockSpec((T, C), lambda b, i: (row(b, i), 0)),
            pl.BlockSpec((T, C), lambda b, i: (row(b, i), 1)),
            pl.BlockSpec((HALO, C), lambda b, i: (halo_row(b, i), 0)),
            pl.BlockSpec((HALO, C), lambda b, i: (halo_row(b, i), 1)),
            pl.BlockSpec((KW, C), lambda b, i: (0, 0)),
            pl.BlockSpec((1, C), lambda b, i: (0, 0)),
            pl.BlockSpec((1, C), lambda b, i: (0, 0)),
            pl.BlockSpec((1, C), lambda b, i: (0, 0)),
        ],
        out_specs=pl.BlockSpec((T, C), lambda b, i: (row(b, i), 0)),
        out_shape=jax.ShapeDtypeStruct((n, C), BF16),
        scratch_shapes=[pltpu.VMEM((HALO + T, C), F32), pltpu.VMEM((T, C), F32),
                        pltpu.VMEM((SUBLANES - 1, HALO + T, LANES), F32)],
        compiler_params=_params("parallel", "parallel"),
        name="conv_branch",
    )(p_conv, p_conv, p_conv, p_conv, conv_w, vec(conv_bias), vec(ln_g), vec(ln_b))


def _index_mask_kernel(q_ref, kt_ref, w_ref, o_ref, sc, wb, rhs, *, TQ, TS, S, G, HG, DI, topk, idx_bits):
    i = pl.program_id(1)
    NH = G * HG
    CR = TQ
    NACC = 4
    int_min = jnp.int32(-2 ** 31)

    for h in range(NH):
        wb[h] = jnp.broadcast_to(w_ref[:, h:h + 1], (TQ, TS))
    rhs[...] = jnp.zeros_like(rhs)
    q = q_ref[...].reshape(G * TQ, HG * DI)
    t_pos_c = i * TQ + lax.broadcasted_iota(jnp.int32, (TS, TQ), 1)
    row_c = lax.broadcasted_iota(jnp.int32, (TS, TQ), 0)

    def score_chunk(c, slot):
        cs = pl.ds(pl.multiple_of(c * TS, TS), TS)
        kt = kt_ref[0, :, cs]
        for h in range(HG):
            rhs[slot, h * DI:(h + 1) * DI, h * TS:(h + 1) * TS] = kt
        x = jnp.dot(q, rhs[slot], preferred_element_type=F32)
        acc = jnp.zeros((TQ, TS), F32)
        for g in range(G):
            for h in range(HG):
                acc = acc + wb[g * HG + h] * jnp.maximum(x[g * TQ:(g + 1) * TQ, h * TS:(h + 1) * TS], 0.0)
        acc = acc.T + 0.0
        bits = pltpu.bitcast(acc, jnp.int32)
        key = jnp.where(bits < 0, bits ^ jnp.int32(0x7FFFFFFF), bits)
        sc[cs, :] = jnp.where(c * TS + row_c <= t_pos_c, key, int_min)

    CPT = TQ // TS

    def score_group(cg, carry):
        for u in range(CPT):
            score_chunk(cg * CPT + u, u)
        return carry

    lax.fori_loop(0, i + 1, score_group, 0)

    t_pos = i * TQ + lax.broadcasted_iota(jnp.int32, (CR, TQ), 1)
    row = lax.broadcasted_iota(jnp.int32, (CR, TQ), 0)
    row3 = row.reshape(CR // SUBLANES, SUBLANES, TQ)

    def count(pred):
        def body(c, cnt):
            k3 = sc[pl.ds(pl.multiple_of(c * CR, CR), CR), :].reshape(CR // SUBLANES, SUBLANES, TQ)
            hit = pred(k3, c * CR + row3).astype(jnp.int32)
            return cnt + jnp.sum(hit.reshape(CR // (SUBLANES * NACC), NACC, SUBLANES, TQ), axis=0)

        cnt = lax.fori_loop(0, i + 1, body, jnp.zeros((NACC, SUBLANES, TQ), jnp.int32))
        return jnp.sum(cnt.reshape(NACC * SUBLANES, TQ), axis=0, keepdims=True)

    bc = lambda v: jnp.broadcast_to(v, (SUBLANES, TQ))[None]

    def bit_step(b, thr):
        cand = thr + jnp.left_shift(jnp.int32(1), 31 - b)
        cand_b = bc(cand)
        n_ge = count(lambda k, s: k >= cand_b)
        return jnp.where(n_ge >= topk, cand, thr)

    thr = lax.fori_loop(0, 32, bit_step, jnp.full((1, TQ), int_min, jnp.int32))
    thr_b = bc(thr)
    n_ge = count(lambda k, s: k >= thr_b)
    has_ties = jnp.max(n_ge) > topk

    def write(sel):
        def body(c, carry):
            rs = pl.ds(pl.multiple_of(c * CR, CR), CR)
            s_pos = c * CR + row
            keep = sel(sc[rs, :], s_pos) & (s_pos <= t_pos)
            o_ref[0, rs, :] = jnp.where(keep, 0.0, MASKED).astype(o_ref.dtype)
            return carry

        lax.fori_loop(0, i + 1, body, 0)

    @pl.when(jnp.logical_not(has_ties))
    def _():
        write(lambda k, s_pos: k >= thr)

    @pl.when(has_ties)
    def _():
        need = bc(topk - count(lambda k, s: k > thr_b))

        def idx_step(b, lim):
            cand = lim + jnp.left_shift(jnp.int32(1), idx_bits - 1 - b)
            cand_b = bc(cand)
            n_before = count(lambda k, s: (k == thr_b) & (s < cand_b))
            return jnp.where(n_before < need[0, 0:1], cand, lim)

        lim = lax.fori_loop(0, idx_bits, idx_step, jnp.zeros((1, TQ), jnp.int32))
        write(lambda k, s_pos: (k > thr) | ((k == thr) & (s_pos <= lim)))

    def clear(c, carry):
        o_ref[0, pl.ds(pl.multiple_of(c * CR, CR), CR), :] = jnp.full((CR, TQ), MASKED, o_ref.dtype)
        return carry

    lax.fori_loop(i + 1, S // CR, clear, 0)


def _index_mask(qi_g, ki_t, wi, batch, seq, topk, TQ=256, TS=128):
    G, n, gw = qi_g.shape
    DI = ki_t.shape[1]
    HG = gw // DI
    NH = G * HG
    TQ = min(TQ, seq)
    TS = min(TS, TQ)
    nq = seq // TQ
    idx_bits = max(1, int(seq - 1).bit_length())
    return pl.pallas_call(
        functools.partial(_index_mask_kernel, TQ=TQ, TS=TS, S=seq, G=G, HG=HG, DI=DI, topk=topk, idx_bits=idx_bits),
        grid=(batch, nq),
        in_specs=[
            pl.BlockSpec((G, TQ, gw), lambda b, i: (0, b * nq + i, 0)),
            pl.BlockSpec((1, DI, seq), lambda b, i: (b, 0, 0)),
            pl.BlockSpec((TQ, NH), lambda b, i: (b * nq + i, 0)),
        ],
        out_specs=pl.BlockSpec((1, seq, TQ), lambda b, i: (b, 0, i)),
        out_shape=jax.ShapeDtypeStruct((batch, seq, seq), BF16),
        scratch_shapes=[
            pltpu.VMEM((seq, TQ), jnp.int32),
            pltpu.VMEM((NH, TQ, TS), F32),
            pltpu.VMEM((TQ // TS, HG * DI, HG * TS), BF16),
        ],
        compiler_params=_params("parallel", "parallel"),
        name="index_mask",
    )(qi_g, ki_t, wi)


def _attn_kernel(ii_ref, jj_ref, qt_ref, k_ref, vt_ref, mask_ref, bias_ref, bfar_ref, o_ref,
                 m_scr, r_scr, a_scr, z_scr, acc_scr, *, H, DH, VA, NEAR, c2):
    pair = pl.program_id(1)
    i = ii_ref[pair]
    j = jj_ref[pair]

    @pl.when(j == 0)
    def _():
        m_scr[...] = jnp.full_like(m_scr, MASKED)
        acc_scr[...] = jnp.zeros_like(acc_scr)

    def scores(near):
        mask = mask_ref[0].astype(F32)
        for h in range(H):
            hs = slice(h * DH, (h + 1) * DH)
            z = jnp.dot(k_ref[:, hs], qt_ref[hs, :], preferred_element_type=F32) + mask
            if near is not None:
                z = z + bias_ref[h, near]
                shift = 0.0
            else:
                shift = bfar_ref[h]
            z_scr[h] = z
            m_old = m_scr[h]
            m_new = jnp.maximum(m_old, jnp.max(z, axis=0, keepdims=True) + shift)
            a_scr[h] = jnp.exp2((m_old - m_new) * c2)
            r_scr[h] = m_new - shift
            m_scr[h] = m_new

    for d in range(NEAR):
        pl.when(i - j == d)(functools.partial(scores, d))
    pl.when(i - j >= NEAR)(functools.partial(scores, None))

    for h in range(H):
        p = jnp.exp2((z_scr[h] - r_scr[h]) * c2)
        pv = jnp.dot(vt_ref[h * VA:(h + 1) * VA, :], p.astype(vt_ref.dtype), preferred_element_type=F32)
        acc_scr[h] = a_scr[h] * acc_scr[h] + pv

    @pl.when(j == i)
    def _():
        for h in range(H):
            a = acc_scr[h]
            o_ref[:, h * DH:(h + 1) * DH] = (a[:DH] / a[DH:DH + 1]).T.astype(o_ref.dtype)


def _attention(q_t, k, vt_aug, mask_t, bias_t, bias_far, batch, seq, n_heads, T):
    width, n = q_t.shape
    DH = width // n_heads
    VA = vt_aug.shape[0] // n_heads
    nt = seq // T
    NEAR = bias_t.shape[1]
    c2 = (DH ** -0.5) * math.log2(math.e)
    pairs = [(i, j) for i in range(nt) for j in range(i + 1)]
    ii = jnp.asarray([p[0] for p in pairs], jnp.int32)
    jj = jnp.asarray([p[1] for p in pairs], jnp.int32)
    stat = pltpu.VMEM((n_heads, 1, T), F32)
    grid_spec = pltpu.PrefetchScalarGridSpec(
        num_scalar_prefetch=2,
        grid=(batch, len(pairs)),
        in_specs=[
            pl.BlockSpec((width, T), lambda b, p, ii, jj: (0, b * nt + ii[p])),
            pl.BlockSpec((T, width), lambda b, p, ii, jj: (b * nt + jj[p], 0)),
            pl.BlockSpec((n_heads * VA, T), lambda b, p, ii, jj: (0, b * nt + jj[p])),
            pl.BlockSpec((1, T, T), lambda b, p, ii, jj: (b, jj[p], ii[p])),
            pl.BlockSpec(bias_t.shape, lambda b, p, ii, jj: (0, 0, 0, 0)),
            pl.BlockSpec(bias_far.shape, lambda b, p, ii, jj: (0, 0, 0)),
        ],
        out_specs=pl.BlockSpec((T, width), lambda b, p, ii, jj: (b * nt + ii[p], 0)),
        scratch_shapes=[stat, stat, stat, pltpu.VMEM((n_heads, T, T), F32), pltpu.VMEM((n_heads, VA, T), F32)],
    )
    return pl.pallas_call(
        functools.partial(_attn_kernel, H=n_heads, DH=DH, VA=VA, NEAR=NEAR, c2=c2),
        grid_spec=grid_spec,
        out_shape=jax.ShapeDtypeStruct((n, width), BF16),
        compiler_params=_params("parallel", "arbitrary"),
        name="masked_attention",
    )(ii, jj, q_t, k, vt_aug, mask_t, bias_t, bias_far)


def _t5_bucket(n):
    max_exact = N_BUCKETS // 2
    nf = jnp.maximum(n, 1).astype(F32)
    large = max_exact + (jnp.log(nf / max_exact) / math.log(MAX_DISTANCE / max_exact)
                         * (N_BUCKETS - max_exact)).astype(jnp.int32)
    large = jnp.minimum(large, N_BUCKETS - 1)
    return jnp.where(n < max_exact, n, large)


def _bias_tiles(rel_bias, T, near, scale):
    s = jnp.arange(T, dtype=jnp.int32)[:, None]
    t = jnp.arange(T, dtype=jnp.int32)[None, :]
    rb = rel_bias.astype(F32) / scale
    tiles = []
    for d in range(near):
        bucket = _t5_bucket(jnp.maximum(d * T + t - s, 0))
        onehot = bucket[None, :, :] == jnp.arange(N_BUCKETS, dtype=jnp.int32)[:, None, None]
        tiles.append(jnp.sum(jnp.where(onehot[:, None], rb[:, :, None, None], 0.0), axis=0))
    far = jnp.broadcast_to(rb[N_BUCKETS - 1][:, None, None], (rb.shape[1], 1, T))
    return jnp.stack(tiles, axis=1), far


def kernel(x, norm1_g, w_in, b_gate, conv_w, conv_bias, conv_ln_g, conv_ln_b, w_conv_out, w_attn_out,
           rel_bias, w_o, norm2_g, w_ff1, w_ff2, normf_g):
    B, S, D = x.shape
    N = B * S
    depth = w_in.shape[0]
    C = conv_w.shape[-1]
    A = w_attn_out.shape[1]
    H = rel_bias.shape[1]
    DH = A // H
    NH, DI = IDX_HEADS, IDX_DIM
    HG = MXU_DIM // DI
    topk = min(TOPK_MAX, S // 4)
    off_q = 2 * C
    off_iq = off_q + 3 * A
    off_ik = off_iq + NH * DI
    off_iw = off_ik + DI
    off_gate = off_iw + NH
    T_ATT = min(256, S)
    near = -(-(MAX_DISTANCE + T_ATT - 1) // T_ATT)
    bias_t, bias_far = _bias_tiles(rel_bias, T_ATT, near, DH ** -0.5)
    sigmoid_bias = lambda acc, b: jax.nn.sigmoid(acc + b)
    relu2 = lambda acc: jnp.square(jnp.maximum(acc, 0.0))
    add = lambda acc, r: acc + r

    h = x.reshape(N, D)
    for l in range(depth):
        wl = w_in[l]
        u = _rmsnorm(h, norm1_g[l], BF16)
        p_conv = _matmul(u, wl[:, :off_q].astype(BF16), F32, name="proj_conv")
        q_t = _matmul_t(u, wl[:, off_q:off_q + A].astype(BF16), BF16, name="proj_q")
        k = _matmul(u, wl[:, off_q + A:off_q + 2 * A].astype(BF16), BF16, name="proj_k")
        vt_aug = _matmul_t(u, wl[:, off_q + 2 * A:off_iq].astype(BF16), BF16, DH=DH, ones_rows=BF16_ROWS,
                           name="proj_v")
        qi_g = _matmul(u, wl[:, off_iq:off_ik].astype(BF16), BF16, group_width=HG * DI, name="proj_qi")
        w_small = jnp.pad(wl[:, off_ik:off_gate], ((0, 0), (0, LANES - (DI + NH)))).astype(BF16)
        kw = _matmul(u, w_small, F32, name="proj_ki_wi")
        gates = _matmul(u, wl[:, off_gate:].astype(BF16), BF16, epilogue=sigmoid_bias,
                        extras=[("row", b_gate[l].reshape(1, -1))], name="proj_gates")

        cf = _conv_branch(p_conv, conv_w[l], conv_bias[l], conv_ln_g[l], conv_ln_b[l], B, S)

        ki_t = jnp.transpose(kw[:, :DI].astype(BF16).reshape(B, S, DI), (0, 2, 1))
        wi = kw[:, DI:DI + NH] * (NH ** -0.5) * (DI ** -0.5)
        mask_t = _index_mask(qi_g, ki_t, wi, B, S, topk)
        o = _attention(q_t, k, vt_aug, mask_t, bias_t, bias_far, B, S, H, T_ATT)

        mixed = _gated_mix(cf, w_conv_out[l].astype(BF16), o, w_attn_out[l].astype(BF16), gates, BF16)
        h = _matmul(mixed, w_o[l].astype(BF16), F32, epilogue=add, extras=[("tile", h, 0)], tn=512, name="w_o")
        u2 = _rmsnorm(h, norm2_g[l], BF16)
        act = _matmul(u2, w_ff1[l].astype(BF16), BF16, epilogue=relu2, name="ff1")
        h = _matmul(act, w_ff2[l].astype(BF16), F32, epilogue=add, extras=[("tile", h, 0)], tn=512, name="ff2")
    return _rmsnorm(h, normf_g, F32).reshape(B, S, D)
```

```python
import functools
import math

import jax
import jax.numpy as jnp
from jax import lax
from jax.experimental import pallas as pl
from jax.experimental.pallas import tpu as pltpu

IDX_HEADS = 32
IDX_DIM = 64
TOPK_MAX = 256
N_BUCKETS = 32
MAX_DISTANCE = 128
EPS = 1e-6

V7X_VMEM_BYTES = 64 * 1024 * 1024
VMEM_LIMIT_BYTES = V7X_VMEM_BYTES - 8 * 1024 * 1024
LANES = 128
SUBLANES = 8
BF16_ROWS = 16
MXU_DIM = 256
MASKED = -(2.0 ** 100)

F32 = jnp.float32
BF16 = jnp.bfloat16


def _params(*sem):
    return pltpu.CompilerParams(dimension_semantics=sem, vmem_limit_bytes=VMEM_LIMIT_BYTES)


def _rmsnorm_kernel(x_ref, g_ref, o_ref):
    x = x_ref[...]
    ms = jnp.mean(x * x, axis=-1, keepdims=True)
    o_ref[...] = (x * lax.rsqrt(ms + EPS) * g_ref[...]).astype(o_ref.dtype)


def _rmsnorm(x, g, out_dtype, tr=256):
    n, d = x.shape
    tr = min(tr, n)
    return pl.pallas_call(
        _rmsnorm_kernel,
        grid=(n // tr,),
        in_specs=[pl.BlockSpec((tr, d), lambda i: (i, 0)), pl.BlockSpec((1, d), lambda i: (0, 0))],
        out_specs=pl.BlockSpec((tr, d), lambda i: (i, 0)),
        out_shape=jax.ShapeDtypeStruct((n, d), out_dtype),
        compiler_params=_params("parallel"),
        name="rmsnorm",
    )(x, g.reshape(1, d))


def _mm_kernel(*refs, nk, n_extra, epilogue, group_width):
    a_ref, w_ref = refs[0], refs[1]
    extra = refs[2:2 + n_extra]
    o_ref = refs[2 + n_extra]

    def emit(acc):
        res = epilogue(acc, *[e[...] for e in extra]).astype(o_ref.dtype)
        if group_width:
            for g in range(res.shape[1] // group_width):
                o_ref[g] = res[:, g * group_width:(g + 1) * group_width]
        else:
            o_ref[...] = res

    if nk == 1:
        emit(jnp.dot(a_ref[...], w_ref[...], preferred_element_type=F32))
        return
    acc_ref = refs[3 + n_extra]
    k = pl.program_id(2)

    @pl.when(k == 0)
    def _():
        acc_ref[...] = jnp.zeros_like(acc_ref)

    acc_ref[...] += jnp.dot(a_ref[...], w_ref[...], preferred_element_type=F32)

    @pl.when(k == nk - 1)
    def _():
        emit(acc_ref[...])


def _matmul(a, w, out_dtype, *, epilogue=None, extras=(), tm=1024, tn=1024, tk=4096, group_width=0,
            name="matmul"):
    m, kd = a.shape
    n = w.shape[1]
    tm, tn, tk = min(tm, m), min(tn, n), min(tk, kd)
    nk = kd // tk
    if epilogue is None:
        epilogue = lambda acc: acc
    in_specs = [pl.BlockSpec((tm, tk), lambda i, j, k: (i, k)), pl.BlockSpec((tk, tn), lambda i, j, k: (k, j))]
    args = [a, w]
    for ex in extras:
        if ex[0] == "row":
            in_specs.append(pl.BlockSpec((1, tn), lambda i, j, k: (0, j)))
        else:
            off = ex[2] // tn
            in_specs.append(pl.BlockSpec((tm, tn), lambda i, j, k, off=off: (i, j + off)))
        args.append(ex[1])
    if group_width:
        gpt = tn // group_width
        out_specs = pl.BlockSpec((gpt, tm, group_width), lambda i, j, k: (j, i, 0))
        out_shape = jax.ShapeDtypeStruct((n // group_width, m, group_width), out_dtype)
    else:
        out_specs = pl.BlockSpec((tm, tn), lambda i, j, k: (i, j))
        out_shape = jax.ShapeDtypeStruct((m, n), out_dtype)
    return pl.pallas_call(
        functools.partial(_mm_kernel, nk=nk, n_extra=len(extras), epilogue=epilogue, group_width=group_width),
        grid=(m // tm, n // tn, nk),
        in_specs=in_specs,
        out_specs=out_specs,
        out_shape=out_shape,
        scratch_shapes=[pltpu.VMEM((tm, tn), F32)] if nk > 1 else [],
        compiler_params=_params("parallel", "parallel", "arbitrary"),
        name=name,
    )(*args)


def _mm_t_kernel(a_ref, w_ref, o_ref, *, DH, ones_rows):
    res = jnp.dot(a_ref[...], w_ref[...], preferred_element_type=F32).T.astype(o_ref.dtype)
    if not ones_rows:
        o_ref[...] = res
        return
    VA = DH + ones_rows
    for h in range(res.shape[0] // DH):
        o_ref[h * VA:h * VA + DH, :] = res[h * DH:(h + 1) * DH, :]
        o_ref[h * VA + DH:(h + 1) * VA, :] = jnp.ones((ones_rows, res.shape[1]), o_ref.dtype)


def _matmul_t(a, w, out_dtype, *, tm=1024, tn=1024, DH=0, ones_rows=0, name="matmul_t"):
    m, kd = a.shape
    n = w.shape[1]
    tm, tn = min(tm, m), min(tn, n)
    rows = tn // DH * (DH + ones_rows) if ones_rows else tn
    return pl.pallas_call(
        functools.partial(_mm_t_kernel, DH=DH, ones_rows=ones_rows),
        grid=(m // tm, n // tn),
        in_specs=[pl.BlockSpec((tm, kd), lambda i, j: (i, 0)), pl.BlockSpec((kd, tn), lambda i, j: (0, j))],
        out_specs=pl.BlockSpec((rows, tm), lambda i, j: (j, i)),
        out_shape=jax.ShapeDtypeStruct((n // tn * rows, m), out_dtype),
        compiler_params=_params("parallel", "parallel"),
        name=name,
    )(a, w)


def _mix_kernel(a1_ref, w1_ref, a2_ref, w2_ref, g1_ref, g2_ref, o_ref):
    y1 = jnp.dot(a1_ref[...], w1_ref[...], preferred_element_type=F32)
    y2 = jnp.dot(a2_ref[...], w2_ref[...], preferred_element_type=F32)
    o_ref[...] = (g1_ref[...].astype(F32) * y1 + g2_ref[...].astype(F32) * y2).astype(o_ref.dtype)


def _gated_mix(a1, w1, a2, w2, gates, out_dtype, tm=1024, tn=1024):
    m, k1 = a1.shape
    k2 = a2.shape[1]
    n = w1.shape[1]
    tm, tn = min(tm, m), min(tn, n)
    nb = n // tn
    return pl.pallas_call(
        _mix_kernel,
        grid=(m // tm, nb),
        in_specs=[
            pl.BlockSpec((tm, k1), lambda i, j: (i, 0)),
            pl.BlockSpec((k1, tn), lambda i, j: (0, j)),
            pl.BlockSpec((tm, k2), lambda i, j: (i, 0)),
            pl.BlockSpec((k2, tn), lambda i, j: (0, j)),
            pl.BlockSpec((tm, tn), lambda i, j: (i, j)),
            pl.BlockSpec((tm, tn), lambda i, j: (i, j + nb)),
        ],
        out_specs=pl.BlockSpec((tm, tn), lambda i, j: (i, j)),
        out_shape=jax.ShapeDtypeStruct((m, n), out_dtype),
        compiler_params=_params("parallel", "parallel"),
        name="gated_mix",
    )(a1, w1, a2, w2, gates, gates)


def _conv_kernel(a_ref, g_ref, ah_ref, gh_ref, w_ref, cb_ref, lng_ref, lnb_ref, o_ref, hbuf, cbuf, sh,
                 *, T, C, KW, HALO, RC):
    i = pl.program_id(1)
    hh = ah_ref[...] * jax.nn.sigmoid(gh_ref[...])
    hbuf[0:HALO, :] = jnp.where(i > 0, hh, 0.0)
    hbuf[HALO:HALO + T, :] = a_ref[...] * jax.nn.sigmoid(g_ref[...])
    base = HALO - (KW - 1)
    NR = T + HALO - SUBLANES

    def chunk(c, carry):
        cs = pl.ds(pl.multiple_of(c * LANES, LANES), LANES)
        for b in range(1, SUBLANES):
            sh[b - 1, 0:NR, :] = hbuf[b:b + NR, cs]
        for r in range(T // RC):
            acc = jnp.zeros((RC, LANES), F32)
            for j in range(KW):
                a8, b = divmod(base + j, SUBLANES)
                r0 = r * RC + a8 * SUBLANES
                src = hbuf[r0:r0 + RC, cs] if b == 0 else sh[b - 1, r0:r0 + RC, :]
                acc = acc + w_ref[j:j + 1, cs] * src
            cbuf[r * RC:(r + 1) * RC, cs] = acc + cb_ref[:, cs]
        return carry

    lax.fori_loop(0, C // LANES, chunk, 0)

    RN = min(32, T)

    def norm(r, carry):
        rs = pl.ds(pl.multiple_of(r * RN, RN), RN)
        x = cbuf[rs, :]
        mu = jnp.mean(x, axis=-1, keepdims=True)
        xc = x - mu
        var = jnp.mean(xc * xc, axis=-1, keepdims=True)
        y = xc * lax.rsqrt(var + EPS) * lng_ref[...] + lnb_ref[...]
        o_ref[rs, :] = (y * jax.nn.sigmoid(y)).astype(o_ref.dtype)
        return carry

    lax.fori_loop(0, T // RN, norm, 0)


def _conv_branch(p_conv, conv_w, conv_bias, ln_g, ln_b, batch, seq, T=256):
    n, c2 = p_conv.shape
    C = c2 // 2
    KW = conv_w.shape[0]
    HALO = 32
    assert KW - 1 <= HALO
    T = min(T, seq)
    RC = min(64, T)
    nt = seq // T
    hb = T // HALO
    row = lambda b, i: b * nt + i
    halo_row = lambda b, i: jnp.maximum((b * nt + i) * hb - 1, 0)
    vec = lambda v: v.reshape(1, C)
    return pl.pallas_call(
        functools.partial(_conv_kernel, T=T, C=C, KW=KW, HALO=HALO, RC=RC),
        grid=(batch, nt),
        in_specs=[
            pl.BlockSpec((T, C), lambda b, i: (row(b, i), 0)),
            pl.BlockSpec((T, C), lambda b, i: (row(b, i), 1)),
            pl.BlockSpec((HALO, C), lambda b, i: (halo_row(b, i), 0)),
            pl.BlockSpec((HALO, C), lambda b, i: (halo_row(b, i), 1)),
            pl.BlockSpec((KW, C), lambda b, i: (0, 0)),
            pl.BlockSpec((1, C), lambda b, i: (0, 0)),
            pl.BlockSpec((1, C), lambda b, i: (0, 0)),
            pl.BlockSpec((1, C), lambda b, i: (0, 0)),
        ],
        out_specs=pl.BlockSpec((T, C), lambda b, i: (row(b, i), 0)),
        out_shape=jax.ShapeDtypeStruct((n, C), BF16),
        scratch_shapes=[pltpu.VMEM((HALO + T, C), F32), pltpu.VMEM((T, C), F32),
                        pltpu.VMEM((SUBLANES - 1, HALO + T, LANES), F32)],
        compiler_params=_params("parallel", "parallel"),
        name="conv_branch",
    )(p_conv, p_conv, p_conv, p_conv, conv_w, vec(conv_bias), vec(ln_g), vec(ln_b))


def _index_mask_kernel(q_ref, kt_ref, w_ref, o_ref, sc, wb, rhs, *, TQ, TS, S, G, HG, DI, GW, topk, idx_bits):
    i = pl.program_id(1)
    NH = G * HG
    CR = GW * TQ
    ntrips = (i + GW) // GW
    NACC = 4
    int_min = jnp.int32(-2 ** 31)

    for h in range(NH):
        wb[h] = jnp.broadcast_to(w_ref[:, h:h + 1], (TQ, TS))
    rhs[...] = jnp.zeros_like(rhs)
    q = q_ref[...].reshape(G * TQ, HG * DI)
    t_pos_c = i * TQ + lax.broadcasted_iota(jnp.int32, (TS, TQ), 1)
    row_c = lax.broadcasted_iota(jnp.int32, (TS, TQ), 0)

    def score_chunk(c, slot):
        cs = pl.ds(pl.multiple_of(c * TS, TS), TS)
        kt = kt_ref[0, :, cs]
        for h in range(HG):
            rhs[slot, h * DI:(h + 1) * DI, h * TS:(h + 1) * TS] = kt
        x = jnp.dot(q, rhs[slot], preferred_element_type=F32)
        acc = jnp.zeros((TQ, TS), F32)
        for g in range(G):
            for h in range(HG):
                acc = acc + wb[g * HG + h] * jnp.maximum(x[g * TQ:(g + 1) * TQ, h * TS:(h + 1) * TS], 0.0)
        acc = acc.T + 0.0
        bits = pltpu.bitcast(acc, jnp.int32)
        key = jnp.where(bits < 0, bits ^ jnp.int32(0x7FFFFFFF), bits)
        sc[cs, :] = jnp.where(c * TS + row_c <= t_pos_c, key, int_min)

    CPT = CR // TS

    def score_group(cg, carry):
        for u in range(CPT):
            score_chunk(cg * CPT + u, u)
        return carry

    lax.fori_loop(0, ntrips, score_group, 0)

    t_pos = i * TQ + lax.broadcasted_iota(jnp.int32, (CR, TQ), 1)
    row = lax.broadcasted_iota(jnp.int32, (CR, TQ), 0)
    row3 = row.reshape(CR // SUBLANES, SUBLANES, TQ)

    def count(pred):
        def body(c, cnt):
            k3 = sc[pl.ds(pl.multiple_of(c * CR, CR), CR), :].reshape(CR // SUBLANES, SUBLANES, TQ)
            hit = pred(k3, c * CR + row3).astype(jnp.int32)
            return cnt + jnp.sum(hit.reshape(CR // (SUBLANES * NACC), NACC, SUBLANES, TQ), axis=0)

        cnt = lax.fori_loop(0, ntrips, body, jnp.zeros((NACC, SUBLANES, TQ), jnp.int32))
        return jnp.sum(cnt.reshape(NACC * SUBLANES, TQ), axis=0, keepdims=True)

    bc = lambda v: jnp.broadcast_to(v, (SUBLANES, TQ))[None]

    def bit_step(b, thr):
        cand = thr + jnp.left_shift(jnp.int32(1), 31 - b)
        cand_b = bc(cand)
        n_ge = count(lambda k, s: k >= cand_b)
        return jnp.where(n_ge >= topk, cand, thr)

    thr = lax.fori_loop(0, 32, bit_step, jnp.full((1, TQ), int_min, jnp.int32))
    thr_b = bc(thr)
    n_ge = count(lambda k, s: k >= thr_b)
    has_ties = jnp.max(jnp.where(thr > int_min, n_ge, 0)) > topk

    def write(sel):
        def body(c, carry):
            rs = pl.ds(pl.multiple_of(c * CR, CR), CR)
            s_pos = c * CR + row
            keep = sel(sc[rs, :], s_pos) & (s_pos <= t_pos)
            o_ref[0, rs, :] = jnp.where(keep, 0.0, MASKED).astype(o_ref.dtype)
            return carry

        lax.fori_loop(0, ntrips, body, 0)

    @pl.when(jnp.logical_not(has_ties))
    def _():
        write(lambda k, s_pos: k >= thr)

    @pl.when(has_ties)
    def _():
        need = bc(topk - count(lambda k, s: k > thr_b))

        def idx_step(b, lim):
            cand = lim + jnp.left_shift(jnp.int32(1), idx_bits - 1 - b)
            cand_b = bc(cand)
            n_before = count(lambda k, s: (k == thr_b) & (s < cand_b))
            return jnp.where(n_before < need[0, 0:1], cand, lim)

        lim = lax.fori_loop(0, idx_bits, idx_step, jnp.zeros((1, TQ), jnp.int32))
        write(lambda k, s_pos: (k > thr) | ((k == thr) & (s_pos <= lim)))

    def clear(c, carry):
        o_ref[0, pl.ds(pl.multiple_of(c * CR, CR), CR), :] = jnp.full((CR, TQ), MASKED, o_ref.dtype)
        return carry

    lax.fori_loop(ntrips, S // CR, clear, 0)


def _index_mask(qi_g, ki_t, wi, batch, seq, topk, TQ=256, TS=128):
    G, n, gw = qi_g.shape
    DI = ki_t.shape[1]
    HG = gw // DI
    NH = G * HG
    TQ = min(TQ, seq)
    TS = min(TS, TQ)
    nq = seq // TQ
    GW = 2 if nq % 2 == 0 else 1
    idx_bits = max(1, int(seq - 1).bit_length())
    return pl.pallas_call(
        functools.partial(_index_mask_kernel, TQ=TQ, TS=TS, S=seq, G=G, HG=HG, DI=DI, GW=GW, topk=topk,
                          idx_bits=idx_bits),
        grid=(batch, nq),
        in_specs=[
            pl.BlockSpec((G, TQ, gw), lambda b, i: (0, b * nq + i, 0)),
            pl.BlockSpec((1, DI, seq), lambda b, i: (b, 0, 0)),
            pl.BlockSpec((TQ, NH), lambda b, i: (b * nq + i, 0)),
        ],
        out_specs=pl.BlockSpec((1, seq, TQ), lambda b, i: (b, 0, i)),
        out_shape=jax.ShapeDtypeStruct((batch, seq, seq), BF16),
        scratch_shapes=[
            pltpu.VMEM((seq, TQ), jnp.int32),
            pltpu.VMEM((NH, TQ, TS), F32),
            pltpu.VMEM((GW * TQ // TS, HG * DI, HG * TS), BF16),
        ],
        compiler_params=_params("parallel", "parallel"),
        name="index_mask",
    )(qi_g, ki_t, wi)


def _attn_kernel(ii_ref, jj_ref, qt_ref, k_ref, vt_ref, mask_ref, bias_ref, bfar_ref, o_ref,
                 m_scr, r_scr, a_scr, z_scr, acc_scr, *, H, DH, VA, NEAR, KT, T, c2):
    pair = pl.program_id(1)
    i = ii_ref[pair]
    jg = jj_ref[pair]

    @pl.when(jg == 0)
    def _():
        m_scr[...] = jnp.full_like(m_scr, MASKED)
        acc_scr[...] = jnp.zeros_like(acc_scr)

    def key_tile(sub, carry):
        j = jg * KT + sub
        ks = pl.ds(pl.multiple_of(sub * T, T), T)

        def scores(near):
            mask = mask_ref[0, ks, :].astype(F32)
            for h in range(H):
                hs = slice(h * DH, (h + 1) * DH)
                z = jnp.dot(k_ref[ks, hs], qt_ref[hs, :], preferred_element_type=F32) * c2 + mask
                m_old = m_scr[h]
                if near is not None:
                    z = z + bias_ref[h, near]
                    r = jnp.maximum(m_old, jnp.max(z, axis=0, keepdims=True)).astype(BF16)
                    m_new = r.astype(F32)
                else:
                    shift = bfar_ref[h]
                    r = (jnp.maximum(m_old, jnp.max(z, axis=0, keepdims=True) + shift) - shift).astype(BF16)
                    m_new = r.astype(F32) + shift
                z_scr[h] = z.astype(BF16)
                a_scr[h] = jnp.exp2(m_old - m_new)
                r_scr[h] = jnp.broadcast_to(r, (BF16_ROWS, T))
                m_scr[h] = m_new

        for d in range(NEAR):
            pl.when(i - j == d)(functools.partial(scores, d))
        pl.when(i - j >= NEAR)(functools.partial(scores, None))

        for h in range(H):
            z3 = z_scr[h].reshape(T // BF16_ROWS, BF16_ROWS, T)
            p = jnp.exp2(z3 - r_scr[h][None]).reshape(T, T)
            pv = jnp.dot(vt_ref[h * VA:(h + 1) * VA, ks], p, preferred_element_type=F32)
            acc_scr[h] = a_scr[h] * acc_scr[h] + pv
        return carry

    lax.fori_loop(0, jnp.minimum(KT, i - jg * KT + 1), key_tile, 0)

    @pl.when(jg * KT + KT > i)
    def _():
        for h in range(H):
            a = acc_scr[h]
            o_ref[:, h * DH:(h + 1) * DH] = (a[:DH] / a[DH:DH + 1]).T.astype(o_ref.dtype)


def _attention(q_t, k, vt_aug, mask_t, bias_t, bias_far, batch, seq, n_heads, T):
    width, n = q_t.shape
    DH = width // n_heads
    VA = vt_aug.shape[0] // n_heads
    nt = seq // T
    NEAR = bias_t.shape[1]
    c2 = (DH ** -0.5) * math.log2(math.e)
    KT = 2 if nt % 2 == 0 else 1
    ng = nt // KT
    pairs = [(i, jg) for i in range(nt) for jg in range(i // KT + 1)]
    ii = jnp.asarray([p[0] for p in pairs], jnp.int32)
    jj = jnp.asarray([p[1] for p in pairs], jnp.int32)
    stat = pltpu.VMEM((n_heads, 1, T), F32)
    grid_spec = pltpu.PrefetchScalarGridSpec(
        num_scalar_prefetch=2,
        grid=(batch, len(pairs)),
        in_specs=[
            pl.BlockSpec((width, T), lambda b, p, ii, jj: (0, b * nt + ii[p])),
            pl.BlockSpec((KT * T, width), lambda b, p, ii, jj: (b * ng + jj[p], 0)),
            pl.BlockSpec((n_heads * VA, KT * T), lambda b, p, ii, jj: (0, b * ng + jj[p])),
            pl.BlockSpec((1, KT * T, T), lambda b, p, ii, jj: (b, jj[p], ii[p])),
            pl.BlockSpec(bias_t.shape, lambda b, p, ii, jj: (0, 0, 0, 0)),
            pl.BlockSpec(bias_far.shape, lambda b, p, ii, jj: (0, 0, 0)),
        ],
        out_specs=pl.BlockSpec((T, width), lambda b, p, ii, jj: (b * nt + ii[p], 0)),
        scratch_shapes=[stat, pltpu.VMEM((n_heads, BF16_ROWS, T), BF16), stat,
                        pltpu.VMEM((n_heads, T, T), BF16), pltpu.VMEM((n_heads, VA, T), F32)],
    )
    return pl.pallas_call(
        functools.partial(_attn_kernel, H=n_heads, DH=DH, VA=VA, NEAR=NEAR, KT=KT, T=T, c2=c2),
        grid_spec=grid_spec,
        out_shape=jax.ShapeDtypeStruct((n, width), BF16),
        compiler_params=_params("parallel", "arbitrary"),
        name="masked_attention",
    )(ii, jj, q_t, k, vt_aug, mask_t, bias_t, bias_far)


def _t5_bucket(n):
    max_exact = N_BUCKETS // 2
    nf = jnp.maximum(n, 1).astype(F32)
    large = max_exact + (jnp.log(nf / max_exact) / math.log(MAX_DISTANCE / max_exact)
                         * (N_BUCKETS - max_exact)).astype(jnp.int32)
    large = jnp.minimum(large, N_BUCKETS - 1)
    return jnp.where(n < max_exact, n, large)


def _bias_tiles(rel_bias, T, near):
    s = jnp.arange(T, dtype=jnp.int32)[:, None]
    t = jnp.arange(T, dtype=jnp.int32)[None, :]
    rb = rel_bias.astype(F32) * math.log2(math.e)
    tiles = []
    for d in range(near):
        bucket = _t5_bucket(jnp.maximum(d * T + t - s, 0))
        onehot = bucket[None, :, :] == jnp.arange(N_BUCKETS, dtype=jnp.int32)[:, None, None]
        tiles.append(jnp.sum(jnp.where(onehot[:, None], rb[:, :, None, None], 0.0), axis=0))
    far = jnp.broadcast_to(rb[N_BUCKETS - 1][:, None, None], (rb.shape[1], 1, T))
    return jnp.stack(tiles, axis=1), far


def kernel(x, norm1_g, w_in, b_gate, conv_w, conv_bias, conv_ln_g, conv_ln_b, w_conv_out, w_attn_out,
           rel_bias, w_o, norm2_g, w_ff1, w_ff2, normf_g):
    B, S, D = x.shape
    N = B * S
    depth = w_in.shape[0]
    C = conv_w.shape[-1]
    A = w_attn_out.shape[1]
    H = rel_bias.shape[1]
    DH = A // H
    NH, DI = IDX_HEADS, IDX_DIM
    HG = MXU_DIM // DI
    topk = min(TOPK_MAX, S // 4)
    off_q = 2 * C
    off_iq = off_q + 3 * A
    off_ik = off_iq + NH * DI
    off_iw = off_ik + DI
    off_gate = off_iw + NH
    T_ATT = min(256, S)
    near = -(-(MAX_DISTANCE + T_ATT - 1) // T_ATT)
    bias_t, bias_far = _bias_tiles(rel_bias, T_ATT, near)
    sigmoid_bias = lambda acc, b: jax.nn.sigmoid(acc + b)
    relu2 = lambda acc: jnp.square(jnp.maximum(acc, 0.0))
    add = lambda acc, r: acc + r

    h = x.reshape(N, D)
    for l in range(depth):
        wl = w_in[l]
        u = _rmsnorm(h, norm1_g[l], BF16)
        p_conv = _matmul(u, wl[:, :off_q].astype(BF16), F32, name="proj_conv")
        q_t = _matmul_t(u, wl[:, off_q:off_q + A].astype(BF16), BF16, name="proj_q")
        k = _matmul(u, wl[:, off_q + A:off_q + 2 * A].astype(BF16), BF16, name="proj_k")
        vt_aug = _matmul_t(u, wl[:, off_q + 2 * A:off_iq].astype(BF16), BF16, DH=DH, ones_rows=BF16_ROWS,
                           name="proj_v")
        qi_g = _matmul(u, wl[:, off_iq:off_ik].astype(BF16), BF16, group_width=HG * DI, name="proj_qi")
        w_small = jnp.pad(wl[:, off_ik:off_gate], ((0, 0), (0, LANES - (DI + NH)))).astype(BF16)
        kw = _matmul(u, w_small, F32, name="proj_ki_wi")
        gates = _matmul(u, wl[:, off_gate:].astype(BF16), BF16, epilogue=sigmoid_bias,
                        extras=[("row", b_gate[l].reshape(1, -1))], name="proj_gates")

        cf = _conv_branch(p_conv, conv_w[l], conv_bias[l], conv_ln_g[l], conv_ln_b[l], B, S)

        ki_t = jnp.transpose(kw[:, :DI].astype(BF16).reshape(B, S, DI), (0, 2, 1))
        wi = kw[:, DI:DI + NH] * (NH ** -0.5) * (DI ** -0.5)
        mask_t = _index_mask(qi_g, ki_t, wi, B, S, topk)
        o = _attention(q_t, k, vt_aug, mask_t, bias_t, bias_far, B, S, H, T_ATT)

        mixed = _gated_mix(cf, w_conv_out[l].astype(BF16), o, w_attn_out[l].astype(BF16), gates, BF16)
        h = _matmul(mixed, w_o[l].astype(BF16), F32, epilogue=add, extras=[("tile", h, 0)], tn=512, name="w_o")
        u2 = _rmsnorm(h, norm2_g[l], BF16)
        act = _matmul(u2, w_ff1[l].astype(BF16), BF16, epilogue=relu2, name="ff1")
        h = _matmul(act, w_ff2[l].astype(BF16), F32, epilogue=add, extras=[("tile", h, 0)], tn=512, name="ff2")
    return _rmsnorm(h, normf_g, F32).reshape(B, S, D)
```

```python
import functools
import math

import jax
import jax.numpy as jnp
from jax import lax
from jax.experimental import pallas as pl
from jax.experimental.pallas import tpu as pltpu

IDX_HEADS = 32
IDX_DIM = 64
TOPK_MAX = 256
N_BUCKETS = 32
MAX_DISTANCE = 128
EPS = 1e-6

V7X_VMEM_BYTES = 64 * 1024 * 1024
VMEM_LIMIT_BYTES = V7X_VMEM_BYTES - 2 * 1024 * 1024
LANES = 128
SUBLANES = 8
BF16_ROWS = 16
MXU_DIM = 256
MASKED = -(2.0 ** 100)

F32 = jnp.float32
BF16 = jnp.bfloat16


def _params(*sem):
    return pltpu.CompilerParams(dimension_semantics=sem, vmem_limit_bytes=VMEM_LIMIT_BYTES)


def _rmsnorm_kernel(x_ref, g_ref, o_ref):
    x = x_ref[...]
    ms = jnp.mean(x * x, axis=-1, keepdims=True)
    o_ref[...] = (x * lax.rsqrt(ms + EPS) * g_ref[...]).astype(o_ref.dtype)


def _rmsnorm(x, g, out_dtype, tr=256):
    n, d = x.shape
    tr = min(tr, n)
    return pl.pallas_call(
        _rmsnorm_kernel,
        grid=(n // tr,),
        in_specs=[pl.BlockSpec((tr, d), lambda i: (i, 0)), pl.BlockSpec((1, d), lambda i: (0, 0))],
        out_specs=pl.BlockSpec((tr, d), lambda i: (i, 0)),
        out_shape=jax.ShapeDtypeStruct((n, d), out_dtype),
        compiler_params=_params("parallel"),
        name="rmsnorm",
    )(x, g.reshape(1, d))


def _mm_kernel(*refs, nk, n_extra, epilogue, group_width):
    a_ref, w_ref = refs[0], refs[1]
    extra = refs[2:2 + n_extra]
    o_ref = refs[2 + n_extra]

    def emit(acc):
        res = epilogue(acc, *[e[...] for e in extra]).astype(o_ref.dtype)
        if group_width:
            for g in range(res.shape[1] // group_width):
                o_ref[g] = res[:, g * group_width:(g + 1) * group_width]
        else:
            o_ref[...] = res

    if nk == 1:
        emit(jnp.dot(a_ref[...], w_ref[...].astype(BF16), preferred_element_type=F32))
        return
    acc_ref = refs[3 + n_extra]
    k = pl.program_id(2)

    @pl.when(k == 0)
    def _():
        acc_ref[...] = jnp.zeros_like(acc_ref)

    acc_ref[...] += jnp.dot(a_ref[...], w_ref[...].astype(BF16), preferred_element_type=F32)

    @pl.when(k == nk - 1)
    def _():
        emit(acc_ref[...])


def _wcols(w, wcols, tn):
    col0, n = wcols if wcols else (0, w.shape[1])
    tn = min(tn, n)
    assert col0 % tn == 0 and n % tn == 0
    return col0 // tn, n, tn


def _matmul(a, w, out_dtype, *, epilogue=None, extras=(), tm=1024, tn=1024, tk=4096, group_width=0, wcols=None,
            name="matmul"):
    m, kd = a.shape
    jb0, n, tn = _wcols(w, wcols, tn)
    tm, tk = min(tm, m), min(tk, kd)
    nk = kd // tk
    if epilogue is None:
        epilogue = lambda acc: acc
    in_specs = [pl.BlockSpec((tm, tk), lambda i, j, k: (i, k)), pl.BlockSpec((tk, tn), lambda i, j, k: (k, j + jb0))]
    args = [a, w]
    for ex in extras:
        if ex[0] == "row":
            in_specs.append(pl.BlockSpec((1, tn), lambda i, j, k: (0, j)))
        else:
            off = ex[2] // tn
            in_specs.append(pl.BlockSpec((tm, tn), lambda i, j, k, off=off: (i, j + off)))
        args.append(ex[1])
    if group_width:
        gpt = tn // group_width
        out_specs = pl.BlockSpec((gpt, tm, group_width), lambda i, j, k: (j, i, 0))
        out_shape = jax.ShapeDtypeStruct((n // group_width, m, group_width), out_dtype)
    else:
        out_specs = pl.BlockSpec((tm, tn), lambda i, j, k: (i, j))
        out_shape = jax.ShapeDtypeStruct((m, n), out_dtype)
    return pl.pallas_call(
        functools.partial(_mm_kernel, nk=nk, n_extra=len(extras), epilogue=epilogue, group_width=group_width),
        grid=(m // tm, n // tn, nk),
        in_specs=in_specs,
        out_specs=out_specs,
        out_shape=out_shape,
        scratch_shapes=[pltpu.VMEM((tm, tn), F32)] if nk > 1 else [],
        compiler_params=_params("parallel", "parallel", "arbitrary"),
        name=name,
    )(*args)


def _mm_t_kernel(a_ref, w_ref, o_ref, *, DH, ones_rows):
    res = jnp.dot(a_ref[...], w_ref[...].astype(BF16), preferred_element_type=F32).T.astype(o_ref.dtype)
    if not ones_rows:
        o_ref[...] = res
        return
    VA = DH + ones_rows
    for h in range(res.shape[0] // DH):
        o_ref[h * VA:h * VA + DH, :] = res[h * DH:(h + 1) * DH, :]
        o_ref[h * VA + DH:(h + 1) * VA, :] = jnp.ones((ones_rows, res.shape[1]), o_ref.dtype)


def _matmul_t(a, w, out_dtype, *, tm=1024, tn=512, DH=0, ones_rows=0, wcols=None, name="matmul_t"):
    m, kd = a.shape
    jb0, n, tn = _wcols(w, wcols, tn)
    tm = min(tm, m)
    rows = tn // DH * (DH + ones_rows) if ones_rows else tn
    return pl.pallas_call(
        functools.partial(_mm_t_kernel, DH=DH, ones_rows=ones_rows),
        grid=(m // tm, n // tn),
        in_specs=[pl.BlockSpec((tm, kd), lambda i, j: (i, 0)), pl.BlockSpec((kd, tn), lambda i, j: (0, j + jb0))],
        out_specs=pl.BlockSpec((rows, tm), lambda i, j: (j, i)),
        out_shape=jax.ShapeDtypeStruct((n // tn * rows, m), out_dtype),
        compiler_params=_params("parallel", "parallel"),
        name=name,
    )(a, w)


def _mix_kernel(a1_ref, w1_ref, a2_ref, w2_ref, g1_ref, g2_ref, o_ref):
    y1 = jnp.dot(a1_ref[...], w1_ref[...].astype(BF16), preferred_element_type=F32)
    y2 = jnp.dot(a2_ref[...], w2_ref[...].astype(BF16), preferred_element_type=F32)
    o_ref[...] = (g1_ref[...].astype(F32) * y1 + g2_ref[...].astype(F32) * y2).astype(o_ref.dtype)


def _gated_mix(a1, w1, a2, w2, gates, out_dtype, tm=1024, tn=512):
    m, k1 = a1.shape
    k2 = a2.shape[1]
    n = w1.shape[1]
    tm, tn = min(tm, m), min(tn, n)
    nb = n // tn
    return pl.pallas_call(
        _mix_kernel,
        grid=(m // tm, nb),
        in_specs=[
            pl.BlockSpec((tm, k1), lambda i, j: (i, 0)),
            pl.BlockSpec((k1, tn), lambda i, j: (0, j)),
            pl.BlockSpec((tm, k2), lambda i, j: (i, 0)),
            pl.BlockSpec((k2, tn), lambda i, j: (0, j)),
            pl.BlockSpec((tm, tn), lambda i, j: (i, j)),
            pl.BlockSpec((tm, tn), lambda i, j: (i, j + nb)),
        ],
        out_specs=pl.BlockSpec((tm, tn), lambda i, j: (i, j)),
        out_shape=jax.ShapeDtypeStruct((m, n), out_dtype),
        compiler_params=_params("parallel", "parallel"),
        name="gated_mix",
    )(a1, w1, a2, w2, gates, gates)


def _conv_kernel(a_ref, g_ref, ah_ref, gh_ref, w_ref, cb_ref, lng_ref, lnb_ref, o_ref, hbuf, cbuf, sh,
                 *, T, C, KW, HALO, RC):
    i = pl.program_id(1)
    hh = ah_ref[...] * jax.nn.sigmoid(gh_ref[...])
    hbuf[0:HALO, :] = jnp.where(i > 0, hh, 0.0)
    hbuf[HALO:HALO + T, :] = a_ref[...] * jax.nn.sigmoid(g_ref[...])
    base = HALO - (KW - 1)
    NR = T + HALO - SUBLANES

    def chunk(c, carry):
        cs = pl.ds(pl.multiple_of(c * LANES, LANES), LANES)
        for b in range(1, SUBLANES):
            sh[b - 1, 0:NR, :] = hbuf[b:b + NR, cs]
        for r in range(T // RC):
            acc = jnp.zeros((RC, LANES), F32)
            for j in range(KW):
                a8, b = divmod(base + j, SUBLANES)
                r0 = r * RC + a8 * SUBLANES
                src = hbuf[r0:r0 + RC, cs] if b == 0 else sh[b - 1, r0:r0 + RC, :]
                acc = acc + w_ref[j:j + 1, cs] * src
            cbuf[r * RC:(r + 1) * RC, cs] = acc + cb_ref[:, cs]
        return carry

    lax.fori_loop(0, C // LANES, chunk, 0)

    RN = min(32, T)

    def norm(r, carry):
        rs = pl.ds(pl.multiple_of(r * RN, RN), RN)
        x = cbuf[rs, :]
        mu = jnp.mean(x, axis=-1, keepdims=True)
        xc = x - mu
        var = jnp.mean(xc * xc, axis=-1, keepdims=True)
        y = xc * lax.rsqrt(var + EPS) * lng_ref[...] + lnb_ref[...]
        o_ref[rs, :] = (y * jax.nn.sigmoid(y)).astype(o_ref.dtype)
        return carry

    lax.fori_loop(0, T // RN, norm, 0)


def _conv_branch(p_conv, conv_w, conv_bias, ln_g, ln_b, batch, seq, T=256):
    n, c2 = p_conv.shape
    C = c2 // 2
    KW = conv_w.shape[0]
    HALO = 32
    assert KW - 1 <= HALO
    T = min(T, seq)
    RC = min(64, T)
    nt = seq // T
    hb = T // HALO
    row = lambda b, i: b * nt + i
    halo_row = lambda b, i: jnp.maximum((b * nt + i) * hb - 1, 0)
    vec = lambda v: v.reshape(1, C)
    return pl.pallas_call(
        functools.partial(_conv_kernel, T=T, C=C, KW=KW, HALO=HALO, RC=RC),
        grid=(batch, nt),
        in_specs=[
            pl.BlockSpec((T, C), lambda b, i: (row(b, i), 0)),
            pl.BlockSpec((T, C), lambda b, i: (row(b, i), 1)),
            pl.BlockSpec((HALO, C), lambda b, i: (halo_row(b, i), 0)),
            pl.BlockSpec((HALO, C), lambda b, i: (halo_row(b, i), 1)),
            pl.BlockSpec((KW, C), lambda b, i: (0, 0)),
            pl.BlockSpec((1, C), lambda b, i: (0, 0)),
            pl.BlockSpec((1, C), lambda b, i: (0, 0)),
            pl.BlockSpec((1, C), lambda b, i: (0, 0)),
        ],
        out_specs=pl.BlockSpec((T, C), lambda b, i: (row(b, i), 0)),
        out_shape=jax.ShapeDtypeStruct((n, C), BF16),
        scratch_shapes=[pltpu.VMEM((HALO + T, C), F32), pltpu.VMEM((T, C), F32),
                        pltpu.VMEM((SUBLANES - 1, HALO + T, LANES), F32)],
        compiler_params=_params("parallel", "parallel"),
        name="conv_branch",
    )(p_conv, p_conv, p_conv, p_conv, conv_w, vec(conv_bias), vec(ln_g), vec(ln_b))


def _index_mask_kernel(q_ref, kt_ref, w_ref, o_ref, sc, wb, rhs, *, TQ, TS, S, G, HG, DI, GW, topk, idx_bits):
    i = pl.program_id(1)
    NH = G * HG
    CR = GW * TQ
    ntrips = (i + GW) // GW
    NACC = 4
    int_min = jnp.int32(-2 ** 31)

    for h in range(NH):
        wb[h] = jnp.broadcast_to(w_ref[:, h:h + 1], (TQ, TS))
    rhs[...] = jnp.zeros_like(rhs)
    q = q_ref[...].reshape(G * TQ, HG * DI)
    t_pos_c = i * TQ + lax.broadcasted_iota(jnp.int32, (TS, TQ), 1)
    row_c = lax.broadcasted_iota(jnp.int32, (TS, TQ), 0)

    def score_chunk(c, slot):
        cs = pl.ds(pl.multiple_of(c * TS, TS), TS)
        kt = kt_ref[0, :, cs]
        for h in range(HG):
            rhs[slot, h * DI:(h + 1) * DI, h * TS:(h + 1) * TS] = kt
        x = jnp.dot(q, rhs[slot], preferred_element_type=F32)
        acc = jnp.zeros((TQ, TS), F32)
        for g in range(G):
            for h in range(HG):
                acc = acc + wb[g * HG + h] * jnp.maximum(x[g * TQ:(g + 1) * TQ, h * TS:(h + 1) * TS], 0.0)
        acc = acc.T + 0.0
        bits = pltpu.bitcast(acc, jnp.int32)
        key = jnp.where(bits < 0, bits ^ jnp.int32(0x7FFFFFFF), bits)
        sc[cs, :] = jnp.where(c * TS + row_c <= t_pos_c, key, int_min)

    CPT = CR // TS

    def score_group(cg, carry):
        for u in range(CPT):
            score_chunk(cg * CPT + u, u)
        return carry

    lax.fori_loop(0, ntrips, score_group, 0)

    t_pos = i * TQ + lax.broadcasted_iota(jnp.int32, (CR, TQ), 1)
    row = lax.broadcasted_iota(jnp.int32, (CR, TQ), 0)
    row3 = row.reshape(CR // SUBLANES, SUBLANES, TQ)

    def count(pred):
        def body(c, cnt):
            k3 = sc[pl.ds(pl.multiple_of(c * CR, CR), CR), :].reshape(CR // SUBLANES, SUBLANES, TQ)
            hit = pred(k3, c * CR + row3).astype(jnp.int32)
            return cnt + jnp.sum(hit.reshape(CR // (SUBLANES * NACC), NACC, SUBLANES, TQ), axis=0)

        cnt = lax.fori_loop(0, ntrips, body, jnp.zeros((NACC, SUBLANES, TQ), jnp.int32))
        return jnp.sum(cnt.reshape(NACC * SUBLANES, TQ), axis=0, keepdims=True)

    bc = lambda v: jnp.broadcast_to(v, (SUBLANES, TQ))[None]

    def bit_step(b, thr):
        cand = thr + jnp.left_shift(jnp.int32(1), 31 - b)
        cand_b = bc(cand)
        n_ge = count(lambda k, s: k >= cand_b)
        return jnp.where(n_ge >= topk, cand, thr)

    thr = lax.fori_loop(0, 32, bit_step, jnp.full((1, TQ), int_min, jnp.int32))
    thr_b = bc(thr)
    n_ge = count(lambda k, s: k >= thr_b)
    has_ties = jnp.max(jnp.where(thr > int_min, n_ge, 0)) > topk

    def write(sel):
        def body(c, carry):
            rs = pl.ds(pl.multiple_of(c * CR, CR), CR)
            s_pos = c * CR + row
            keep = sel(sc[rs, :], s_pos) & (s_pos <= t_pos)
            o_ref[0, rs, :] = jnp.where(keep, 0.0, MASKED).astype(o_ref.dtype)
            return carry

        lax.fori_loop(0, ntrips, body, 0)

    @pl.when(jnp.logical_not(has_ties))
    def _():
        write(lambda k, s_pos: k >= thr)

    @pl.when(has_ties)
    def _():
        need = bc(topk - count(lambda k, s: k > thr_b))

        def idx_step(b, lim):
            cand = lim + jnp.left_shift(jnp.int32(1), idx_bits - 1 - b)
            cand_b = bc(cand)
            n_before = count(lambda k, s: (k == thr_b) & (s < cand_b))
            return jnp.where(n_before < need[0, 0:1], cand, lim)

        lim = lax.fori_loop(0, idx_bits, idx_step, jnp.zeros((1, TQ), jnp.int32))
        write(lambda k, s_pos: (k > thr) | ((k == thr) & (s_pos <= lim)))

    def clear(c, carry):
        o_ref[0, pl.ds(pl.multiple_of(c * CR, CR), CR), :] = jnp.full((CR, TQ), MASKED, o_ref.dtype)
        return carry

    lax.fori_loop(ntrips, S // CR, clear, 0)


def _index_mask(qi_g, ki_t, wi, batch, seq, topk, TQ=256, TS=128):
    G, n, gw = qi_g.shape
    DI = ki_t.shape[1]
    HG = gw // DI
    NH = G * HG
    TQ = min(TQ, seq)
    TS = min(TS, TQ)
    nq = seq // TQ
    GW = 2 if nq % 2 == 0 else 1
    idx_bits = max(1, int(seq - 1).bit_length())
    return pl.pallas_call(
        functools.partial(_index_mask_kernel, TQ=TQ, TS=TS, S=seq, G=G, HG=HG, DI=DI, GW=GW, topk=topk,
                          idx_bits=idx_bits),
        grid=(batch, nq),
        in_specs=[
            pl.BlockSpec((G, TQ, gw), lambda b, i: (0, b * nq + i, 0)),
            pl.BlockSpec((1, DI, seq), lambda b, i: (b, 0, 0)),
            pl.BlockSpec((TQ, NH), lambda b, i: (b * nq + i, 0)),
        ],
        out_specs=pl.BlockSpec((1, seq, TQ), lambda b, i: (b, 0, i)),
        out_shape=jax.ShapeDtypeStruct((batch, seq, seq), BF16),
        scratch_shapes=[
            pltpu.VMEM((seq, TQ), jnp.int32),
            pltpu.VMEM((NH, TQ, TS), F32),
            pltpu.VMEM((GW * TQ // TS, HG * DI, HG * TS), BF16),
        ],
        compiler_params=_params("parallel", "parallel"),
        name="index_mask",
    )(qi_g, ki_t, wi)


def _attn_kernel(ii_ref, jj_ref, qt_ref, k_ref, vt_ref, mask_ref, bias_ref, bfar_ref, o_ref,
                 m_scr, r_scr, a_scr, z_scr, acc_scr, *, H, DH, VA, NEAR, KT, T, c2):
    pair = pl.program_id(1)
    i = ii_ref[pair]
    jg = jj_ref[pair]

    @pl.when(jg == 0)
    def _():
        m_scr[...] = jnp.full_like(m_scr, MASKED)
        acc_scr[...] = jnp.zeros_like(acc_scr)

    def key_tile(sub, carry):
        j = jg * KT + sub
        ks = pl.ds(pl.multiple_of(sub * T, T), T)

        def scores(near):
            mask = mask_ref[0, ks, :].astype(F32)
            for h in range(H):
                hs = slice(h * DH, (h + 1) * DH)
                z = jnp.dot(k_ref[ks, hs], qt_ref[hs, :], preferred_element_type=F32) * c2 + mask
                m_old = m_scr[h]
                if near is not None:
                    z = z + bias_ref[h, near]
                    r = jnp.maximum(m_old, jnp.max(z, axis=0, keepdims=True)).astype(BF16)
                    m_new = r.astype(F32)
                else:
                    shift = bfar_ref[h]
                    r = (jnp.maximum(m_old, jnp.max(z, axis=0, keepdims=True) + shift) - shift).astype(BF16)
                    m_new = r.astype(F32) + shift
                z_scr[h] = z.astype(BF16)
                a_scr[h] = jnp.exp2(m_old - m_new)
                r_scr[h] = jnp.broadcast_to(r, (BF16_ROWS, T))
                m_scr[h] = m_new

        for d in range(NEAR):
            pl.when(i - j == d)(functools.partial(scores, d))
        pl.when(i - j >= NEAR)(functools.partial(scores, None))

        for h in range(H):
            z3 = z_scr[h].reshape(T // BF16_ROWS, BF16_ROWS, T)
            p = jnp.exp2(z3 - r_scr[h][None]).reshape(T, T)
            pv = jnp.dot(vt_ref[h * VA:(h + 1) * VA, ks], p, preferred_element_type=F32)
            acc_scr[h] = a_scr[h] * acc_scr[h] + pv
        return carry

    lax.fori_loop(0, jnp.minimum(KT, i - jg * KT + 1), key_tile, 0)

    @pl.when(jg * KT + KT > i)
    def _():
        for h in range(H):
            a = acc_scr[h]
            o_ref[:, h * DH:(h + 1) * DH] = (a[:DH] / a[DH:DH + 1]).T.astype(o_ref.dtype)


def _attention(q_t, k, vt_aug, mask_t, bias_t, bias_far, batch, seq, n_heads, T):
    width, n = q_t.shape
    DH = width // n_heads
    VA = vt_aug.shape[0] // n_heads
    nt = seq // T
    NEAR = bias_t.shape[1]
    c2 = (DH ** -0.5) * math.log2(math.e)
    KT = 2 if nt % 2 == 0 else 1
    ng = nt // KT
    pairs = [(i, jg) for i in range(nt) for jg in range(i // KT + 1)]
    ii = jnp.asarray([p[0] for p in pairs], jnp.int32)
    jj = jnp.asarray([p[1] for p in pairs], jnp.int32)
    stat = pltpu.VMEM((n_heads, 1, T), F32)
    grid_spec = pltpu.PrefetchScalarGridSpec(
        num_scalar_prefetch=2,
        grid=(batch, len(pairs)),
        in_specs=[
            pl.BlockSpec((width, T), lambda b, p, ii, jj: (0, b * nt + ii[p])),
            pl.BlockSpec((KT * T, width), lambda b, p, ii, jj: (b * ng + jj[p], 0)),
            pl.BlockSpec((n_heads * VA, KT * T), lambda b, p, ii, jj: (0, b * ng + jj[p])),
            pl.BlockSpec((1, KT * T, T), lambda b, p, ii, jj: (b, jj[p], ii[p])),
            pl.BlockSpec(bias_t.shape, lambda b, p, ii, jj: (0, 0, 0, 0)),
            pl.BlockSpec(bias_far.shape, lambda b, p, ii, jj: (0, 0, 0)),
        ],
        out_specs=pl.BlockSpec((T, width), lambda b, p, ii, jj: (b * nt + ii[p], 0)),
        scratch_shapes=[stat, pltpu.VMEM((n_heads, BF16_ROWS, T), BF16), stat,
                        pltpu.VMEM((n_heads, T, T), BF16), pltpu.VMEM((n_heads, VA, T), F32)],
    )
    return pl.pallas_call(
        functools.partial(_attn_kernel, H=n_heads, DH=DH, VA=VA, NEAR=NEAR, KT=KT, T=T, c2=c2),
        grid_spec=grid_spec,
        out_shape=jax.ShapeDtypeStruct((n, width), BF16),
        compiler_params=_params("parallel", "arbitrary"),
        name="masked_attention",
    )(ii, jj, q_t, k, vt_aug, mask_t, bias_t, bias_far)


def _t5_bucket(n):
    max_exact = N_BUCKETS // 2
    nf = jnp.maximum(n, 1).astype(F32)
    large = max_exact + (jnp.log(nf / max_exact) / math.log(MAX_DISTANCE / max_exact)
                         * (N_BUCKETS - max_exact)).astype(jnp.int32)
    large = jnp.minimum(large, N_BUCKETS - 1)
    return jnp.where(n < max_exact, n, large)


def _bias_tiles(rel_bias, T, near):
    s = jnp.arange(T, dtype=jnp.int32)[:, None]
    t = jnp.arange(T, dtype=jnp.int32)[None, :]
    rb = rel_bias.astype(F32) * math.log2(math.e)
    tiles = []
    for d in range(near):
        bucket = _t5_bucket(jnp.maximum(d * T + t - s, 0))
        onehot = bucket[None, :, :] == jnp.arange(N_BUCKETS, dtype=jnp.int32)[:, None, None]
        tiles.append(jnp.sum(jnp.where(onehot[:, None], rb[:, :, None, None], 0.0), axis=0))
    far = jnp.broadcast_to(rb[N_BUCKETS - 1][:, None, None], (rb.shape[1], 1, T))
    return jnp.stack(tiles, axis=1), far


def kernel(x, norm1_g, w_in, b_gate, conv_w, conv_bias, conv_ln_g, conv_ln_b, w_conv_out, w_attn_out,
           rel_bias, w_o, norm2_g, w_ff1, w_ff2, normf_g):
    B, S, D = x.shape
    N = B * S
    depth = w_in.shape[0]
    C = conv_w.shape[-1]
    A = w_attn_out.shape[1]
    H = rel_bias.shape[1]
    DH = A // H
    NH, DI = IDX_HEADS, IDX_DIM
    HG = MXU_DIM // DI
    topk = min(TOPK_MAX, S // 4)
    off_q = 2 * C
    off_iq = off_q + 3 * A
    off_ik = off_iq + NH * DI
    off_iw = off_ik + DI
    off_gate = off_iw + NH
    T_ATT = min(256, S)
    near = -(-(MAX_DISTANCE + T_ATT - 1) // T_ATT)
    bias_t, bias_far = _bias_tiles(rel_bias, T_ATT, near)
    sigmoid_bias = lambda acc, b: jax.nn.sigmoid(acc + b)
    relu2 = lambda acc: jnp.square(jnp.maximum(acc, 0.0))
    add = lambda acc, r: acc + r

    h = x.reshape(N, D)
    for l in range(depth):
        wl = w_in[l]
        u = _rmsnorm(h, norm1_g[l], BF16)
        p_conv = _matmul(u, wl, F32, wcols=(0, off_q), tn=512, name="proj_conv")
        q_t = _matmul_t(u, wl, BF16, wcols=(off_q, A), name="proj_q")
        k = _matmul(u, wl, BF16, wcols=(off_q + A, A), name="proj_k")
        vt_aug = _matmul_t(u, wl, BF16, wcols=(off_q + 2 * A, A), DH=DH, ones_rows=BF16_ROWS, name="proj_v")
        qi_g = _matmul(u, wl, BF16, wcols=(off_iq, NH * DI), group_width=HG * DI, name="proj_qi")
        w_small = jnp.pad(wl[:, off_ik:off_gate], ((0, 0), (0, LANES - (DI + NH)))).astype(BF16)
        kw = _matmul(u, w_small, F32, name="proj_ki_wi")
        gates = _matmul(u, wl[:, off_gate:].astype(BF16), BF16, epilogue=sigmoid_bias,
                        extras=[("row", b_gate[l].reshape(1, -1))], name="proj_gates")

        cf = _conv_branch(p_conv, conv_w[l], conv_bias[l], conv_ln_g[l], conv_ln_b[l], B, S)

        ki_t = jnp.transpose(kw[:, :DI].astype(BF16).reshape(B, S, DI), (0, 2, 1))
        wi = kw[:, DI:DI + NH] * (NH ** -0.5) * (DI ** -0.5)
        mask_t = _index_mask(qi_g, ki_t, wi, B, S, topk)
        o = _attention(q_t, k, vt_aug, mask_t, bias_t, bias_far, B, S, H, T_ATT)

        mixed = _gated_mix(cf, w_conv_out[l], o, w_attn_out[l], gates, BF16)
        h = _matmul(mixed, w_o[l], F32, epilogue=add, extras=[("tile", h, 0)], tn=512, name="w_o")
        u2 = _rmsnorm(h, norm2_g[l], BF16)
        act = _matmul(u2, w_ff1[l], BF16, epilogue=relu2, name="ff1")
        h = _matmul(act, w_ff2[l], F32, epilogue=add, extras=[("tile", h, 0)], tn=512, name="ff2")
    return _rmsnorm(h, normf_g, F32).reshape(B, S, D)
```

```python
import functools
import math

import jax
import jax.numpy as jnp
from jax import lax
from jax.experimental import pallas as pl
from jax.experimental.pallas import tpu as pltpu

IDX_HEADS = 32
IDX_DIM = 64
TOPK_MAX = 256
N_BUCKETS = 32
MAX_DISTANCE = 128
EPS = 1e-6

V7X_VMEM_BYTES = 64 * 1024 * 1024
VMEM_LIMIT_BYTES = V7X_VMEM_BYTES - 2 * 1024 * 1024
LANES = 128
SUBLANES = 8
BF16_ROWS = 16
MXU_DIM = 256
MASKED = -(2.0 ** 100)

F32 = jnp.float32
BF16 = jnp.bfloat16


def _params(*sem):
    return pltpu.CompilerParams(dimension_semantics=sem, vmem_limit_bytes=VMEM_LIMIT_BYTES)


def _rmsnorm_kernel(x_ref, g_ref, o_ref):
    x = x_ref[...]
    ms = jnp.mean(x * x, axis=-1, keepdims=True)
    o_ref[...] = (x * lax.rsqrt(ms + EPS) * g_ref[...]).astype(o_ref.dtype)


def _rmsnorm(x, g, out_dtype, tr=256):
    n, d = x.shape
    tr = min(tr, n)
    return pl.pallas_call(
        _rmsnorm_kernel,
        grid=(n // tr,),
        in_specs=[pl.BlockSpec((tr, d), lambda i: (i, 0)), pl.BlockSpec((1, d), lambda i: (0, 0))],
        out_specs=pl.BlockSpec((tr, d), lambda i: (i, 0)),
        out_shape=jax.ShapeDtypeStruct((n, d), out_dtype),
        compiler_params=_params("parallel"),
        name="rmsnorm",
    )(x, g.reshape(1, d))


def _mm_kernel(*refs, nk, n_extra, epilogue, group_width):
    a_ref, w_ref = refs[0], refs[1]
    extra = refs[2:2 + n_extra]
    o_ref = refs[2 + n_extra]

    def emit(acc):
        res = epilogue(acc, *[e[...] for e in extra]).astype(o_ref.dtype)
        if group_width:
            for g in range(res.shape[1] // group_width):
                o_ref[g] = res[:, g * group_width:(g + 1) * group_width]
        else:
            o_ref[...] = res

    if nk == 1:
        emit(jnp.dot(a_ref[...], w_ref[...].astype(BF16), preferred_element_type=F32))
        return
    acc_ref = refs[3 + n_extra]
    k = pl.program_id(2)

    @pl.when(k == 0)
    def _():
        acc_ref[...] = jnp.zeros_like(acc_ref)

    acc_ref[...] += jnp.dot(a_ref[...], w_ref[...].astype(BF16), preferred_element_type=F32)

    @pl.when(k == nk - 1)
    def _():
        emit(acc_ref[...])


def _wcols(w, wcols, tn):
    col0, n = wcols if wcols else (0, w.shape[1])
    tn = min(tn, n)
    assert col0 % tn == 0 and n % tn == 0
    return col0 // tn, n, tn


def _matmul(a, w, out_dtype, *, epilogue=None, extras=(), tm=1024, tn=1024, tk=4096, group_width=0, wcols=None,
            name="matmul"):
    m, kd = a.shape
    jb0, n, tn = _wcols(w, wcols, tn)
    tm, tk = min(tm, m), min(tk, kd)
    nk = kd // tk
    if epilogue is None:
        epilogue = lambda acc: acc
    in_specs = [pl.BlockSpec((tm, tk), lambda i, j, k: (i, k)), pl.BlockSpec((tk, tn), lambda i, j, k: (k, j + jb0))]
    args = [a, w]
    for ex in extras:
        if ex[0] == "row":
            in_specs.append(pl.BlockSpec((1, tn), lambda i, j, k: (0, j)))
        else:
            off = ex[2] // tn
            in_specs.append(pl.BlockSpec((tm, tn), lambda i, j, k, off=off: (i, j + off)))
        args.append(ex[1])
    if group_width:
        gpt = tn // group_width
        out_specs = pl.BlockSpec((gpt, tm, group_width), lambda i, j, k: (j, i, 0))
        out_shape = jax.ShapeDtypeStruct((n // group_width, m, group_width), out_dtype)
    else:
        out_specs = pl.BlockSpec((tm, tn), lambda i, j, k: (i, j))
        out_shape = jax.ShapeDtypeStruct((m, n), out_dtype)
    return pl.pallas_call(
        functools.partial(_mm_kernel, nk=nk, n_extra=len(extras), epilogue=epilogue, group_width=group_width),
        grid=(m // tm, n // tn, nk),
        in_specs=in_specs,
        out_specs=out_specs,
        out_shape=out_shape,
        scratch_shapes=[pltpu.VMEM((tm, tn), F32)] if nk > 1 else [],
        compiler_params=_params("parallel", "parallel", "arbitrary"),
        name=name,
    )(*args)


def _mm_t_kernel(a_ref, w_ref, o_ref, *, DH, ones_rows):
    res = jnp.dot(a_ref[...], w_ref[...].astype(BF16), preferred_element_type=F32).T.astype(o_ref.dtype)
    if not ones_rows:
        o_ref[...] = res
        return
    VA = DH + ones_rows
    for h in range(res.shape[0] // DH):
        o_ref[h * VA:h * VA + DH, :] = res[h * DH:(h + 1) * DH, :]
        o_ref[h * VA + DH:(h + 1) * VA, :] = jnp.ones((ones_rows, res.shape[1]), o_ref.dtype)


def _matmul_t(a, w, out_dtype, *, tm=1024, tn=1024, DH=0, ones_rows=0, wcols=None, name="matmul_t"):
    m, kd = a.shape
    jb0, n, tn = _wcols(w, wcols, tn)
    tm = min(tm, m)
    rows = tn // DH * (DH + ones_rows) if ones_rows else tn
    return pl.pallas_call(
        functools.partial(_mm_t_kernel, DH=DH, ones_rows=ones_rows),
        grid=(m // tm, n // tn),
        in_specs=[pl.BlockSpec((tm, kd), lambda i, j: (i, 0)), pl.BlockSpec((kd, tn), lambda i, j: (0, j + jb0))],
        out_specs=pl.BlockSpec((rows, tm), lambda i, j: (j, i)),
        out_shape=jax.ShapeDtypeStruct((n // tn * rows, m), out_dtype),
        compiler_params=_params("parallel", "parallel"),
        name=name,
    )(a, w)


def _mix_kernel(a1_ref, w1_ref, a2_ref, w2_ref, g1_ref, g2_ref, o_ref):
    y1 = jnp.dot(a1_ref[...], w1_ref[...].astype(BF16), preferred_element_type=F32)
    y2 = jnp.dot(a2_ref[...], w2_ref[...].astype(BF16), preferred_element_type=F32)
    o_ref[...] = (g1_ref[...].astype(F32) * y1 + g2_ref[...].astype(F32) * y2).astype(o_ref.dtype)


def _gated_mix(a1, w1, a2, w2, gates, out_dtype, tm=1024, tn=1024):
    m, k1 = a1.shape
    k2 = a2.shape[1]
    n = w1.shape[1]
    tm, tn = min(tm, m), min(tn, n)
    nb = n // tn
    return pl.pallas_call(
        _mix_kernel,
        grid=(m // tm, nb),
        in_specs=[
            pl.BlockSpec((tm, k1), lambda i, j: (i, 0)),
            pl.BlockSpec((k1, tn), lambda i, j: (0, j)),
            pl.BlockSpec((tm, k2), lambda i, j: (i, 0)),
            pl.BlockSpec((k2, tn), lambda i, j: (0, j)),
            pl.BlockSpec((tm, tn), lambda i, j: (i, j)),
            pl.BlockSpec((tm, tn), lambda i, j: (i, j + nb)),
        ],
        out_specs=pl.BlockSpec((tm, tn), lambda i, j: (i, j)),
        out_shape=jax.ShapeDtypeStruct((m, n), out_dtype),
        compiler_params=_params("parallel", "parallel"),
        name="gated_mix",
    )(a1, w1, a2, w2, gates, gates)


def _conv_kernel(a_ref, g_ref, ah_ref, gh_ref, w_ref, cb_ref, lng_ref, lnb_ref, o_ref, hbuf, cbuf, sh,
                 *, T, C, KW, HALO, RC):
    i = pl.program_id(1)
    hh = ah_ref[...] * jax.nn.sigmoid(gh_ref[...])
    hbuf[0:HALO, :] = jnp.where(i > 0, hh, 0.0)
    hbuf[HALO:HALO + T, :] = a_ref[...] * jax.nn.sigmoid(g_ref[...])
    base = HALO - (KW - 1)
    NR = T + HALO - SUBLANES

    def chunk(c, carry):
        cs = pl.ds(pl.multiple_of(c * LANES, LANES), LANES)
        for b in range(1, SUBLANES):
            sh[b - 1, 0:NR, :] = hbuf[b:b + NR, cs]
        for r in range(T // RC):
            acc = jnp.zeros((RC, LANES), F32)
            for j in range(KW):
                a8, b = divmod(base + j, SUBLANES)
                r0 = r * RC + a8 * SUBLANES
                src = hbuf[r0:r0 + RC, cs] if b == 0 else sh[b - 1, r0:r0 + RC, :]
                acc = acc + w_ref[j:j + 1, cs] * src
            cbuf[r * RC:(r + 1) * RC, cs] = acc + cb_ref[:, cs]
        return carry

    lax.fori_loop(0, C // LANES, chunk, 0)

    RN = min(32, T)

    def norm(r, carry):
        rs = pl.ds(pl.multiple_of(r * RN, RN), RN)
        x = cbuf[rs, :]
        mu = jnp.mean(x, axis=-1, keepdims=True)
        xc = x - mu
        var = jnp.mean(xc * xc, axis=-1, keepdims=True)
        y = xc * lax.rsqrt(var + EPS) * lng_ref[...] + lnb_ref[...]
        o_ref[rs, :] = (y * jax.nn.sigmoid(y)).astype(o_ref.dtype)
        return carry

    lax.fori_loop(0, T // RN, norm, 0)


def _conv_branch(p_conv, conv_w, conv_bias, ln_g, ln_b, batch, seq, T=256):
    n, c2 = p_conv.shape
    C = c2 // 2
    KW = conv_w.shape[0]
    HALO = 32
    assert KW - 1 <= HALO
    T = min(T, seq)
    RC = min(64, T)
    nt = seq // T
    hb = T // HALO
    row = lambda b, i: b * nt + i
    halo_row = lambda b, i: jnp.maximum((b * nt + i) * hb - 1, 0)
    vec = lambda v: v.reshape(1, C)
    return pl.pallas_call(
        functools.partial(_conv_kernel, T=T, C=C, KW=KW, HALO=HALO, RC=RC),
        grid=(batch, nt),
        in_specs=[
            pl.BlockSpec((T, C), lambda b, i: (row(b, i), 0)),
            pl.BlockSpec((T, C), lambda b, i: (row(b, i), 1)),
            pl.BlockSpec((HALO, C), lambda b, i: (halo_row(b, i), 0)),
            pl.BlockSpec((HALO, C), lambda b, i: (halo_row(b, i), 1)),
            pl.BlockSpec((KW, C), lambda b, i: (0, 0)),
            pl.BlockSpec((1, C), lambda b, i: (0, 0)),
            pl.BlockSpec((1, C), lambda b, i: (0, 0)),
            pl.BlockSpec((1, C), lambda b, i: (0, 0)),
        ],
        out_specs=pl.BlockSpec((T, C), lambda b, i: (row(b, i), 0)),
        out_shape=jax.ShapeDtypeStruct((n, C), BF16),
        scratch_shapes=[pltpu.VMEM((HALO + T, C), F32), pltpu.VMEM((T, C), F32),
                        pltpu.VMEM((SUBLANES - 1, HALO + T, LANES), F32)],
        compiler_params=_params("parallel", "parallel"),
        name="conv_branch",
    )(p_conv, p_conv, p_conv, p_conv, conv_w, vec(conv_bias), vec(ln_g), vec(ln_b))


def _index_mask_kernel(q_ref, kt_ref, w_ref, o_ref, sc, hi16, lo16, wb, rhs,
                       *, TQ, TS, S, G, HG, DI, GW, topk, idx_bits):
    i = pl.program_id(1)
    NH = G * HG
    CR = GW * TQ
    ntrips = (i + GW) // GW
    NACC = 4
    int_min = jnp.int32(-2 ** 31)

    for h in range(NH):
        wb[h] = jnp.broadcast_to(w_ref[:, h:h + 1], (TQ, TS))
    rhs[...] = jnp.zeros_like(rhs)
    q = q_ref[...].reshape(G * TQ, HG * DI)
    t_pos_c = i * TQ + lax.broadcasted_iota(jnp.int32, (TS, TQ), 1)
    row_c = lax.broadcasted_iota(jnp.int32, (TS, TQ), 0)

    def score_chunk(c, slot):
        cs = pl.ds(pl.multiple_of(c * TS, TS), TS)
        kt = kt_ref[0, :, cs]
        for h in range(HG):
            rhs[slot, h * DI:(h + 1) * DI, h * TS:(h + 1) * TS] = kt
        x = jnp.dot(q, rhs[slot], preferred_element_type=F32)
        acc = jnp.zeros((TQ, TS), F32)
        for g in range(G):
            for h in range(HG):
                acc = acc + wb[g * HG + h] * jnp.maximum(x[g * TQ:(g + 1) * TQ, h * TS:(h + 1) * TS], 0.0)
        acc = acc.T + 0.0
        bits = pltpu.bitcast(acc, jnp.int32)
        key = jnp.where(bits < 0, bits ^ jnp.int32(0x7FFFFFFF), bits)
        key = jnp.where(c * TS + row_c <= t_pos_c, key, int_min)
        sc[cs, :] = key
        hi16[cs, :] = jnp.right_shift(key, 16).astype(jnp.int16)

    CPT = CR // TS

    def score_group(cg, carry):
        for u in range(CPT):
            score_chunk(cg * CPT + u, u)
        return carry

    lax.fori_loop(0, ntrips, score_group, 0)

    t_pos = i * TQ + lax.broadcasted_iota(jnp.int32, (CR, TQ), 1)
    row = lax.broadcasted_iota(jnp.int32, (CR, TQ), 0)
    row3 = row.reshape(CR // SUBLANES, SUBLANES, TQ)

    def count(pred):
        def body(c, cnt):
            k3 = sc[pl.ds(pl.multiple_of(c * CR, CR), CR), :].reshape(CR // SUBLANES, SUBLANES, TQ)
            hit = pred(k3, c * CR + row3).astype(jnp.int32)
            return cnt + jnp.sum(hit.reshape(CR // (SUBLANES * NACC), NACC, SUBLANES, TQ), axis=0)

        cnt = lax.fori_loop(0, ntrips, body, jnp.zeros((NACC, SUBLANES, TQ), jnp.int32))
        return jnp.sum(cnt.reshape(NACC * SUBLANES, TQ), axis=0, keepdims=True)

    bc = lambda v: jnp.broadcast_to(v, (SUBLANES, TQ))[None]
    bc16 = lambda v: jnp.broadcast_to(v, (BF16_ROWS, TQ)).astype(jnp.int16)[None]
    G16 = CR // (BF16_ROWS * NACC)
    assert (S // CR) * G16 < 2 ** 15
    min16 = -2 ** 15

    def count16(ref, cand):
        cand_b = bc16(cand)

        def body(c, cnt):
            k3 = ref[pl.ds(pl.multiple_of(c * CR, CR), CR), :].reshape(CR // BF16_ROWS, BF16_ROWS, TQ)
            hit = jnp.where(k3 >= cand_b, jnp.int16(1), jnp.int16(0)).reshape(G16, NACC, BF16_ROWS, TQ)
            for g in range(G16):
                cnt = cnt + hit[g]
            return cnt

        cnt = lax.fori_loop(0, ntrips, body, jnp.zeros((NACC, BF16_ROWS, TQ), jnp.int16))
        return jnp.sum(cnt.astype(jnp.int32).reshape(NACC * BF16_ROWS, TQ), axis=0, keepdims=True)

    def search16(ref, need):
        def bit_step(b, cur):
            cand = cur + jnp.left_shift(jnp.int32(1), 15 - b)
            return jnp.where(count16(ref, cand) >= need, cand, cur)

        return lax.fori_loop(0, 16, bit_step, jnp.full((1, TQ), min16, jnp.int32))

    thr_hi = search16(hi16, topk)
    n_above = jnp.where(thr_hi == 2 ** 15 - 1, 0, count16(hi16, jnp.minimum(thr_hi + 1, 2 ** 15 - 1)))

    def fill_lo(c, carry):
        rs = pl.ds(pl.multiple_of(c * CR, CR), CR)
        k = sc[rs, :]
        lo = jnp.bitwise_and(k, 0xFFFF) + min16
        lo16[rs, :] = jnp.where(jnp.right_shift(k, 16) == thr_hi, lo, min16).astype(jnp.int16)
        return carry

    lax.fori_loop(0, ntrips, fill_lo, 0)
    thr_lo = search16(lo16, topk - n_above)
    thr = thr_hi * 2 ** 16 + (thr_lo - min16)
    thr_b = bc(thr)
    n_ge = count(lambda k, s: k >= thr_b)
    has_ties = jnp.max(jnp.where(thr > int_min, n_ge, 0)) > topk

    def write(sel):
        def body(c, carry):
            rs = pl.ds(pl.multiple_of(c * CR, CR), CR)
            s_pos = c * CR + row
            keep = sel(sc[rs, :], s_pos) & (s_pos <= t_pos)
            o_ref[0, rs, :] = jnp.where(keep, 0.0, MASKED).astype(o_ref.dtype)
            return carry

        lax.fori_loop(0, ntrips, body, 0)

    @pl.when(jnp.logical_not(has_ties))
    def _():
        write(lambda k, s_pos: k >= thr)

    @pl.when(has_ties)
    def _():
        need = bc(topk - count(lambda k, s: k > thr_b))

        def idx_step(b, lim):
            cand = lim + jnp.left_shift(jnp.int32(1), idx_bits - 1 - b)
            cand_b = bc(cand)
            n_before = count(lambda k, s: (k == thr_b) & (s < cand_b))
            return jnp.where(n_before < need[0, 0:1], cand, lim)

        lim = lax.fori_loop(0, idx_bits, idx_step, jnp.zeros((1, TQ), jnp.int32))
        write(lambda k, s_pos: (k > thr) | ((k == thr) & (s_pos <= lim)))

    def clear(c, carry):
        o_ref[0, pl.ds(pl.multiple_of(c * CR, CR), CR), :] = jnp.full((CR, TQ), MASKED, o_ref.dtype)
        return carry

    lax.fori_loop(ntrips, S // CR, clear, 0)


def _index_mask(qi_g, ki_t, wi, batch, seq, topk, TQ=256, TS=128):
    G, n, gw = qi_g.shape
    DI = ki_t.shape[1]
    HG = gw // DI
    NH = G * HG
    TQ = min(TQ, seq)
    TS = min(TS, TQ)
    nq = seq // TQ
    GW = 2 if nq % 2 == 0 else 1
    idx_bits = max(1, int(seq - 1).bit_length())
    return pl.pallas_call(
        functools.partial(_index_mask_kernel, TQ=TQ, TS=TS, S=seq, G=G, HG=HG, DI=DI, GW=GW, topk=topk,
                          idx_bits=idx_bits),
        grid=(batch, nq),
        in_specs=[
            pl.BlockSpec((G, TQ, gw), lambda b, i: (0, b * nq + i, 0)),
            pl.BlockSpec((1, DI, seq), lambda b, i: (b, 0, 0)),
            pl.BlockSpec((TQ, NH), lambda b, i: (b * nq + i, 0)),
        ],
        out_specs=pl.BlockSpec((1, seq, TQ), lambda b, i: (b, 0, i)),
        out_shape=jax.ShapeDtypeStruct((batch, seq, seq), BF16),
        scratch_shapes=[
            pltpu.VMEM((seq, TQ), jnp.int32),
            pltpu.VMEM((seq, TQ), jnp.int16),
            pltpu.VMEM((seq, TQ), jnp.int16),
            pltpu.VMEM((NH, TQ, TS), F32),
            pltpu.VMEM((GW * TQ // TS, HG * DI, HG * TS), BF16),
        ],
        compiler_params=_params("parallel", "parallel"),
        name="index_mask",
    )(qi_g, ki_t, wi)


def _attn_kernel(ii_ref, jj_ref, qt_ref, k_ref, vt_ref, mask_ref, bias_ref, bfar_ref, o_ref,
                 m_scr, r_scr, a_scr, z_scr, acc_scr, *, H, DH, VA, NEAR, KT, T, c2):
    pair = pl.program_id(1)
    i = ii_ref[pair]
    jg = jj_ref[pair]

    @pl.when(jg == 0)
    def _():
        m_scr[...] = jnp.full_like(m_scr, MASKED)
        acc_scr[...] = jnp.zeros_like(acc_scr)

    def key_tile(sub, carry):
        j = jg * KT + sub
        ks = pl.ds(pl.multiple_of(sub * T, T), T)

        def scores(near):
            mask = mask_ref[0, ks, :].astype(F32)
            for h in range(H):
                hs = slice(h * DH, (h + 1) * DH)
                z = jnp.dot(k_ref[ks, hs], qt_ref[hs, :], preferred_element_type=F32) * c2 + mask
                m_old = m_scr[h]
                if near is not None:
                    z = z + bias_ref[h, near]
                    r = jnp.maximum(m_old, jnp.max(z, axis=0, keepdims=True)).astype(BF16)
                    m_new = r.astype(F32)
                else:
                    shift = bfar_ref[h]
                    r = (jnp.maximum(m_old, jnp.max(z, axis=0, keepdims=True) + shift) - shift).astype(BF16)
                    m_new = r.astype(F32) + shift
                z_scr[h] = z.astype(BF16)
                a_scr[h] = jnp.exp2(m_old - m_new)
                r_scr[h] = jnp.broadcast_to(r, (BF16_ROWS, T))
                m_scr[h] = m_new

        for d in range(NEAR):
            pl.when(i - j == d)(functools.partial(scores, d))
        pl.when(i - j >= NEAR)(functools.partial(scores, None))

        for h in range(H):
            z3 = z_scr[h].reshape(T // BF16_ROWS, BF16_ROWS, T)
            p = jnp.exp2(z3 - r_scr[h][None]).reshape(T, T)
            pv = jnp.dot(vt_ref[h * VA:(h + 1) * VA, ks], p, preferred_element_type=F32)
            acc_scr[h] = a_scr[h] * acc_scr[h] + pv
        return carry

    lax.fori_loop(0, jnp.minimum(KT, i - jg * KT + 1), key_tile, 0)

    @pl.when(jg * KT + KT > i)
    def _():
        for h in range(H):
            a = acc_scr[h]
            o_ref[:, h * DH:(h + 1) * DH] = (a[:DH] / a[DH:DH + 1]).T.astype(o_ref.dtype)


def _attention(q_t, k, vt_aug, mask_t, bias_t, bias_far, batch, seq, n_heads, T):
    width, n = q_t.shape
    DH = width // n_heads
    VA = vt_aug.shape[0] // n_heads
    nt = seq // T
    NEAR = bias_t.shape[1]
    c2 = (DH ** -0.5) * math.log2(math.e)
    KT = 2 if nt % 2 == 0 else 1
    ng = nt // KT
    pairs = [(i, jg) for i in range(nt) for jg in range(i // KT + 1)]
    ii = jnp.asarray([p[0] for p in pairs], jnp.int32)
    jj = jnp.asarray([p[1] for p in pairs], jnp.int32)
    stat = pltpu.VMEM((n_heads, 1, T), F32)
    grid_spec = pltpu.PrefetchScalarGridSpec(
        num_scalar_prefetch=2,
        grid=(batch, len(pairs)),
        in_specs=[
            pl.BlockSpec((width, T), lambda b, p, ii, jj: (0, b * nt + ii[p])),
            pl.BlockSpec((KT * T, width), lambda b, p, ii, jj: (b * ng + jj[p], 0)),
            pl.BlockSpec((n_heads * VA, KT * T), lambda b, p, ii, jj: (0, b * ng + jj[p])),
            pl.BlockSpec((1, KT * T, T), lambda b, p, ii, jj: (b, jj[p], ii[p])),
            pl.BlockSpec(bias_t.shape, lambda b, p, ii, jj: (0, 0, 0, 0)),
            pl.BlockSpec(bias_far.shape, lambda b, p, ii, jj: (0, 0, 0)),
        ],
        out_specs=pl.BlockSpec((T, width), lambda b, p, ii, jj: (b * nt + ii[p], 0)),
        scratch_shapes=[stat, pltpu.VMEM((n_heads, BF16_ROWS, T), BF16), stat,
                        pltpu.VMEM((n_heads, T, T), BF16), pltpu.VMEM((n_heads, VA, T), F32)],
    )
    return pl.pallas_call(
        functools.partial(_attn_kernel, H=n_heads, DH=DH, VA=VA, NEAR=NEAR, KT=KT, T=T, c2=c2),
        grid_spec=grid_spec,
        out_shape=jax.ShapeDtypeStruct((n, width), BF16),
        compiler_params=_params("parallel", "arbitrary"),
        name="masked_attention",
    )(ii, jj, q_t, k, vt_aug, mask_t, bias_t, bias_far)


def _t5_bucket(n):
    max_exact = N_BUCKETS // 2
    nf = jnp.maximum(n, 1).astype(F32)
    large = max_exact + (jnp.log(nf / max_exact) / math.log(MAX_DISTANCE / max_exact)
                         * (N_BUCKETS - max_exact)).astype(jnp.int32)
    large = jnp.minimum(large, N_BUCKETS - 1)
    return jnp.where(n < max_exact, n, large)


def _bias_tiles(rel_bias, T, near):
    s = jnp.arange(T, dtype=jnp.int32)[:, None]
    t = jnp.arange(T, dtype=jnp.int32)[None, :]
    rb = rel_bias.astype(F32) * math.log2(math.e)
    tiles = []
    for d in range(near):
        bucket = _t5_bucket(jnp.maximum(d * T + t - s, 0))
        onehot = bucket[None, :, :] == jnp.arange(N_BUCKETS, dtype=jnp.int32)[:, None, None]
        tiles.append(jnp.sum(jnp.where(onehot[:, None], rb[:, :, None, None], 0.0), axis=0))
    far = jnp.broadcast_to(rb[N_BUCKETS - 1][:, None, None], (rb.shape[1], 1, T))
    return jnp.stack(tiles, axis=1), far


def kernel(x, norm1_g, w_in, b_gate, conv_w, conv_bias, conv_ln_g, conv_ln_b, w_conv_out, w_attn_out,
           rel_bias, w_o, norm2_g, w_ff1, w_ff2, normf_g):
    B, S, D = x.shape
    N = B * S
    depth = w_in.shape[0]
    C = conv_w.shape[-1]
    A = w_attn_out.shape[1]
    H = rel_bias.shape[1]
    DH = A // H
    NH, DI = IDX_HEADS, IDX_DIM
    HG = MXU_DIM // DI
    topk = min(TOPK_MAX, S // 4)
    off_q = 2 * C
    off_iq = off_q + 3 * A
    off_ik = off_iq + NH * DI
    off_iw = off_ik + DI
    off_gate = off_iw + NH
    T_ATT = min(256, S)
    near = -(-(MAX_DISTANCE + T_ATT - 1) // T_ATT)
    bias_t, bias_far = _bias_tiles(rel_bias, T_ATT, near)
    sigmoid_bias = lambda acc, b: jax.nn.sigmoid(acc + b)
    relu2 = lambda acc: jnp.square(jnp.maximum(acc, 0.0))
    add = lambda acc, r: acc + r

    h = x.reshape(N, D)
    for l in range(depth):
        wl = w_in[l]
        u = _rmsnorm(h, norm1_g[l], BF16)
        wb16 = wl[:, :off_ik].astype(BF16)
        p_conv = _matmul(u, wb16, F32, wcols=(0, off_q), name="proj_conv")
        q_t = _matmul_t(u, wb16, BF16, wcols=(off_q, A), name="proj_q")
        k = _matmul(u, wb16, BF16, wcols=(off_q + A, A), name="proj_k")
        vt_aug = _matmul_t(u, wb16, BF16, wcols=(off_q + 2 * A, A), DH=DH, ones_rows=BF16_ROWS, name="proj_v")
        qi_g = _matmul(u, wb16, BF16, wcols=(off_iq, NH * DI), group_width=HG * DI, name="proj_qi")
        w_small = jnp.pad(wl[:, off_ik:off_gate], ((0, 0), (0, LANES - (DI + NH)))).astype(BF16)
        kw = _matmul(u, w_small, F32, name="proj_ki_wi")
        gates = _matmul(u, wl[:, off_gate:].astype(BF16), BF16, epilogue=sigmoid_bias,
                        extras=[("row", b_gate[l].reshape(1, -1))], name="proj_gates")

        cf = _conv_branch(p_conv, conv_w[l], conv_bias[l], conv_ln_g[l], conv_ln_b[l], B, S)

        ki_t = jnp.transpose(kw[:, :DI].astype(BF16).reshape(B, S, DI), (0, 2, 1))
        wi = kw[:, DI:DI + NH] * (NH ** -0.5) * (DI ** -0.5)
        mask_t = _index_mask(qi_g, ki_t, wi, B, S, topk)
        o = _attention(q_t, k, vt_aug, mask_t, bias_t, bias_far, B, S, H, T_ATT)

        mixed = _gated_mix(cf, w_conv_out[l].astype(BF16), o, w_attn_out[l].astype(BF16), gates, BF16)
        h = _matmul(mixed, w_o[l].astype(BF16), F32, epilogue=add, extras=[("tile", h, 0)], tn=512, name="w_o")
        u2 = _rmsnorm(h, norm2_g[l], BF16)
        act = _matmul(u2, w_ff1[l], BF16, epilogue=relu2, name="ff1")
        h = _matmul(act, w_ff2[l].astype(BF16), F32, epilogue=add, extras=[("tile", h, 0)], tn=512, name="ff2")
    return _rmsnorm(h, normf_g, F32).reshape(B, S, D)
```

```python
import functools
import math

import jax
import jax.numpy as jnp
from jax import lax
from jax.experimental import pallas as pl
from jax.experimental.pallas import tpu as pltpu

IDX_HEADS = 32
IDX_DIM = 64
TOPK_MAX = 256
N_BUCKETS = 32
MAX_DISTANCE = 128
EPS = 1e-6

V7X_VMEM_BYTES = 64 * 1024 * 1024
VMEM_LIMIT_BYTES = V7X_VMEM_BYTES - 2 * 1024 * 1024
LANES = 128
SUBLANES = 8
BF16_ROWS = 16
MXU_DIM = 256
MASKED = -(2.0 ** 100)

F32 = jnp.float32
BF16 = jnp.bfloat16


def _params(*sem):
    return pltpu.CompilerParams(dimension_semantics=sem, vmem_limit_bytes=VMEM_LIMIT_BYTES)


def _rmsnorm_kernel(x_ref, g_ref, o_ref):
    x = x_ref[...]
    ms = jnp.mean(x * x, axis=-1, keepdims=True)
    o_ref[...] = (x * lax.rsqrt(ms + EPS) * g_ref[...]).astype(o_ref.dtype)


def _rmsnorm(x, g, out_dtype, tr=256):
    n, d = x.shape
    tr = min(tr, n)
    return pl.pallas_call(
        _rmsnorm_kernel,
        grid=(n // tr,),
        in_specs=[pl.BlockSpec((tr, d), lambda i: (i, 0)), pl.BlockSpec((1, d), lambda i: (0, 0))],
        out_specs=pl.BlockSpec((tr, d), lambda i: (i, 0)),
        out_shape=jax.ShapeDtypeStruct((n, d), out_dtype),
        compiler_params=_params("parallel"),
        name="rmsnorm",
    )(x, g.reshape(1, d))


def _mm_kernel(*refs, nk, n_extra, epilogue, group_width):
    a_ref, w_ref = refs[0], refs[1]
    extra = refs[2:2 + n_extra]
    o_ref = refs[2 + n_extra]

    def emit(acc):
        res = epilogue(acc, *[e[...] for e in extra]).astype(o_ref.dtype)
        if group_width:
            for g in range(res.shape[1] // group_width):
                o_ref[g] = res[:, g * group_width:(g + 1) * group_width]
        else:
            o_ref[...] = res

    if nk == 1:
        emit(jnp.dot(a_ref[...], w_ref[...].astype(BF16), preferred_element_type=F32))
        return
    acc_ref = refs[3 + n_extra]
    k = pl.program_id(2)

    @pl.when(k == 0)
    def _():
        acc_ref[...] = jnp.zeros_like(acc_ref)

    acc_ref[...] += jnp.dot(a_ref[...], w_ref[...].astype(BF16), preferred_element_type=F32)

    @pl.when(k == nk - 1)
    def _():
        emit(acc_ref[...])


def _wcols(w, wcols, tn):
    col0, n = wcols if wcols else (0, w.shape[1])
    tn = min(tn, n)
    assert col0 % tn == 0 and n % tn == 0
    return col0 // tn, n, tn


def _matmul(a, w, out_dtype, *, epilogue=None, extras=(), tm=1024, tn=1024, tk=4096, group_width=0, wcols=None,
            name="matmul"):
    m, kd = a.shape
    jb0, n, tn = _wcols(w, wcols, tn)
    tm, tk = min(tm, m), min(tk, kd)
    nk = kd // tk
    if epilogue is None:
        epilogue = lambda acc: acc
    in_specs = [pl.BlockSpec((tm, tk), lambda i, j, k: (i, k)), pl.BlockSpec((tk, tn), lambda i, j, k: (k, j + jb0))]
    args = [a, w]
    for ex in extras:
        if ex[0] == "row":
            in_specs.append(pl.BlockSpec((1, tn), lambda i, j, k: (0, j)))
        else:
            off = ex[2] // tn
            in_specs.append(pl.BlockSpec((tm, tn), lambda i, j, k, off=off: (i, j + off)))
        args.append(ex[1])
    if group_width:
        gpt = tn // group_width
        out_specs = pl.BlockSpec((gpt, tm, group_width), lambda i, j, k: (j, i, 0))
        out_shape = jax.ShapeDtypeStruct((n // group_width, m, group_width), out_dtype)
    else:
        out_specs = pl.BlockSpec((tm, tn), lambda i, j, k: (i, j))
        out_shape = jax.ShapeDtypeStruct((m, n), out_dtype)
    return pl.pallas_call(
        functools.partial(_mm_kernel, nk=nk, n_extra=len(extras), epilogue=epilogue, group_width=group_width),
        grid=(m // tm, n // tn, nk),
        in_specs=in_specs,
        out_specs=out_specs,
        out_shape=out_shape,
        scratch_shapes=[pltpu.VMEM((tm, tn), F32)] if nk > 1 else [],
        compiler_params=_params("parallel", "parallel", "arbitrary"),
        name=name,
    )(*args)


def _mm_t_kernel(a_ref, w_ref, o_ref, *, DH, ones_rows):
    res = jnp.dot(a_ref[...], w_ref[...].astype(BF16), preferred_element_type=F32).T.astype(o_ref.dtype)
    if not ones_rows:
        o_ref[...] = res
        return
    VA = DH + ones_rows
    for h in range(res.shape[0] // DH):
        o_ref[h * VA:h * VA + DH, :] = res[h * DH:(h + 1) * DH, :]
        o_ref[h * VA + DH:(h + 1) * VA, :] = jnp.ones((ones_rows, res.shape[1]), o_ref.dtype)


def _matmul_t(a, w, out_dtype, *, tm=1024, tn=1024, DH=0, ones_rows=0, wcols=None, name="matmul_t"):
    m, kd = a.shape
    jb0, n, tn = _wcols(w, wcols, tn)
    tm = min(tm, m)
    rows = tn // DH * (DH + ones_rows) if ones_rows else tn
    return pl.pallas_call(
        functools.partial(_mm_t_kernel, DH=DH, ones_rows=ones_rows),
        grid=(m // tm, n // tn),
        in_specs=[pl.BlockSpec((tm, kd), lambda i, j: (i, 0)), pl.BlockSpec((kd, tn), lambda i, j: (0, j + jb0))],
        out_specs=pl.BlockSpec((rows, tm), lambda i, j: (j, i)),
        out_shape=jax.ShapeDtypeStruct((n // tn * rows, m), out_dtype),
        compiler_params=_params("parallel", "parallel"),
        name=name,
    )(a, w)


def _mix_kernel(a1_ref, w1_ref, a2_ref, w2_ref, g1_ref, g2_ref, o_ref):
    y1 = jnp.dot(a1_ref[...], w1_ref[...].astype(BF16), preferred_element_type=F32)
    y2 = jnp.dot(a2_ref[...], w2_ref[...].astype(BF16), preferred_element_type=F32)
    o_ref[...] = (g1_ref[...].astype(F32) * y1 + g2_ref[...].astype(F32) * y2).astype(o_ref.dtype)


def _gated_mix(a1, w1, a2, w2, gates, out_dtype, tm=1024, tn=1024):
    m, k1 = a1.shape
    k2 = a2.shape[1]
    n = w1.shape[1]
    tm, tn = min(tm, m), min(tn, n)
    nb = n // tn
    return pl.pallas_call(
        _mix_kernel,
        grid=(m // tm, nb),
        in_specs=[
            pl.BlockSpec((tm, k1), lambda i, j: (i, 0)),
            pl.BlockSpec((k1, tn), lambda i, j: (0, j)),
            pl.BlockSpec((tm, k2), lambda i, j: (i, 0)),
            pl.BlockSpec((k2, tn), lambda i, j: (0, j)),
            pl.BlockSpec((tm, tn), lambda i, j: (i, j)),
            pl.BlockSpec((tm, tn), lambda i, j: (i, j + nb)),
        ],
        out_specs=pl.BlockSpec((tm, tn), lambda i, j: (i, j)),
        out_shape=jax.ShapeDtypeStruct((m, n), out_dtype),
        compiler_params=_params("parallel", "parallel"),
        name="gated_mix",
    )(a1, w1, a2, w2, gates, gates)


def _conv_kernel(a_ref, g_ref, ah_ref, gh_ref, w_ref, cb_ref, lng_ref, lnb_ref, o_ref, hbuf, cbuf, sh,
                 *, T, C, KW, HALO, RC):
    i = pl.program_id(1)
    hh = ah_ref[...] * jax.nn.sigmoid(gh_ref[...])
    hbuf[0:HALO, :] = jnp.where(i > 0, hh, 0.0)
    hbuf[HALO:HALO + T, :] = a_ref[...] * jax.nn.sigmoid(g_ref[...])
    base = HALO - (KW - 1)
    NR = T + HALO - SUBLANES

    def chunk(c, carry):
        cs = pl.ds(pl.multiple_of(c * LANES, LANES), LANES)
        for b in range(1, SUBLANES):
            sh[b - 1, 0:NR, :] = hbuf[b:b + NR, cs]
        for r in range(T // RC):
            acc = jnp.zeros((RC, LANES), F32)
            for j in range(KW):
                a8, b = divmod(base + j, SUBLANES)
                r0 = r * RC + a8 * SUBLANES
                src = hbuf[r0:r0 + RC, cs] if b == 0 else sh[b - 1, r0:r0 + RC, :]
                acc = acc + w_ref[j:j + 1, cs] * src
            cbuf[r * RC:(r + 1) * RC, cs] = acc + cb_ref[:, cs]
        return carry

    lax.fori_loop(0, C // LANES, chunk, 0)

    RN = min(64, T)

    def norm(r, carry):
        rs = pl.ds(pl.multiple_of(r * RN, RN), RN)
        x = cbuf[rs, :]
        mu = jnp.mean(x, axis=-1, keepdims=True)
        xc = x - mu
        var = jnp.mean(xc * xc, axis=-1, keepdims=True)
        y = xc * lax.rsqrt(var + EPS) * lng_ref[...] + lnb_ref[...]
        o_ref[rs, :] = (y * jax.nn.sigmoid(y)).astype(o_ref.dtype)
        return carry

    lax.fori_loop(0, T // RN, norm, 0)


def _conv_branch(p_conv, conv_w, conv_bias, ln_g, ln_b, batch, seq, T=256):
    n, c2 = p_conv.shape
    C = c2 // 2
    KW = conv_w.shape[0]
    HALO = 32
    assert KW - 1 <= HALO
    T = min(T, seq)
    RC = min(64, T)
    nt = seq // T
    hb = T // HALO
    row = lambda b, i: b * nt + i
    halo_row = lambda b, i: jnp.maximum((b * nt + i) * hb - 1, 0)
    vec = lambda v: v.reshape(1, C)
    return pl.pallas_call(
        functools.partial(_conv_kernel, T=T, C=C, KW=KW, HALO=HALO, RC=RC),
        grid=(batch, nt),
        in_specs=[
            pl.BlockSpec((T, C), lambda b, i: (row(b, i), 0)),
            pl.BlockSpec((T, C), lambda b, i: (row(b, i), 1)),
            pl.BlockSpec((HALO, C), lambda b, i: (halo_row(b, i), 0)),
            pl.BlockSpec((HALO, C), lambda b, i: (halo_row(b, i), 1)),
            pl.BlockSpec((KW, C), lambda b, i: (0, 0)),
            pl.BlockSpec((1, C), lambda b, i: (0, 0)),
            pl.BlockSpec((1, C), lambda b, i: (0, 0)),
            pl.BlockSpec((1, C), lambda b, i: (0, 0)),
        ],
        out_specs=pl.BlockSpec((T, C), lambda b, i: (row(b, i), 0)),
        out_shape=jax.ShapeDtypeStruct((n, C), BF16),
        scratch_shapes=[pltpu.VMEM((HALO + T, C), F32), pltpu.VMEM((T, C), F32),
                        pltpu.VMEM((SUBLANES - 1, HALO + T, LANES), F32)],
        compiler_params=_params("parallel", "parallel"),
        name="conv_branch",
    )(p_conv, p_conv, p_conv, p_conv, conv_w, vec(conv_bias), vec(ln_g), vec(ln_b))


def _index_mask_kernel(q_ref, kt_ref, w_ref, o_ref, sc, hi16, lo16, wb, rhs,
                       *, TQ, TS, S, G, HG, DI, GW, topk, idx_bits):
    i = pl.program_id(1)
    NH = G * HG
    CR = GW * TQ
    ntrips = (i + GW) // GW
    NACC = 4
    int_min = jnp.int32(-2 ** 31)

    for h in range(NH):
        wb[h] = jnp.broadcast_to(w_ref[:, h:h + 1], (TQ, TS))
    rhs[...] = jnp.zeros_like(rhs)
    q = q_ref[...].reshape(G * TQ, HG * DI)
    t_pos_c = i * TQ + lax.broadcasted_iota(jnp.int32, (TS, TQ), 1)
    row_c = lax.broadcasted_iota(jnp.int32, (TS, TQ), 0)

    def score_chunk(c, slot):
        cs = pl.ds(pl.multiple_of(c * TS, TS), TS)
        kt = kt_ref[0, :, cs]
        for h in range(HG):
            rhs[slot, h * DI:(h + 1) * DI, h * TS:(h + 1) * TS] = kt
        x = jnp.dot(q, rhs[slot], preferred_element_type=F32)
        acc = jnp.zeros((TQ, TS), F32)
        for g in range(G):
            for h in range(HG):
                acc = acc + wb[g * HG + h] * jnp.maximum(x[g * TQ:(g + 1) * TQ, h * TS:(h + 1) * TS], 0.0)
        acc = acc.T + 0.0
        bits = pltpu.bitcast(acc, jnp.int32)
        key = jnp.where(bits < 0, bits ^ jnp.int32(0x7FFFFFFF), bits)
        key = jnp.where(c * TS + row_c <= t_pos_c, key, int_min)
        sc[cs, :] = key
        hi16[cs, :] = jnp.right_shift(key, 16).astype(jnp.int16)

    CPT = CR // TS

    def score_group(cg, carry):
        for u in range(CPT):
            score_chunk(cg * CPT + u, u)
        return carry

    lax.fori_loop(0, ntrips, score_group, 0)

    t_pos = i * TQ + lax.broadcasted_iota(jnp.int32, (CR, TQ), 1)
    row = lax.broadcasted_iota(jnp.int32, (CR, TQ), 0)
    row3 = row.reshape(CR // SUBLANES, SUBLANES, TQ)

    def count(pred):
        def body(c, cnt):
            k3 = sc[pl.ds(pl.multiple_of(c * CR, CR), CR), :].reshape(CR // SUBLANES, SUBLANES, TQ)
            hit = pred(k3, c * CR + row3).astype(jnp.int32)
            return cnt + jnp.sum(hit.reshape(CR // (SUBLANES * NACC), NACC, SUBLANES, TQ), axis=0)

        cnt = lax.fori_loop(0, ntrips, body, jnp.zeros((NACC, SUBLANES, TQ), jnp.int32))
        return jnp.sum(cnt.reshape(NACC * SUBLANES, TQ), axis=0, keepdims=True)

    bc = lambda v: jnp.broadcast_to(v, (SUBLANES, TQ))[None]
    bc16 = lambda v: jnp.broadcast_to(v, (BF16_ROWS, TQ)).astype(jnp.int16)[None]
    G16 = CR // (BF16_ROWS * NACC)
    assert (S // CR) * G16 < 2 ** 15
    min16 = -2 ** 15

    def count16(ref, cand):
        cand_b = bc16(cand)

        def body(c, cnt):
            k3 = ref[pl.ds(pl.multiple_of(c * CR, CR), CR), :].reshape(CR // BF16_ROWS, BF16_ROWS, TQ)
            hit = jnp.where(k3 >= cand_b, jnp.int16(1), jnp.int16(0)).reshape(G16, NACC, BF16_ROWS, TQ)
            for g in range(G16):
                cnt = cnt + hit[g]
            return cnt

        cnt = lax.fori_loop(0, ntrips, body, jnp.zeros((NACC, BF16_ROWS, TQ), jnp.int16))
        return jnp.sum(cnt.astype(jnp.int32).reshape(NACC * BF16_ROWS, TQ), axis=0, keepdims=True)

    def search16(ref, need):
        def bit_step(b, cur):
            cand = cur + jnp.left_shift(jnp.int32(1), 15 - b)
            return jnp.where(count16(ref, cand) >= need, cand, cur)

        return lax.fori_loop(0, 16, bit_step, jnp.full((1, TQ), min16, jnp.int32))

    thr_hi = search16(hi16, topk)
    n_above = jnp.where(thr_hi == 2 ** 15 - 1, 0, count16(hi16, jnp.minimum(thr_hi + 1, 2 ** 15 - 1)))

    def fill_lo(c, carry):
        rs = pl.ds(pl.multiple_of(c * CR, CR), CR)
        k = sc[rs, :]
        lo = jnp.bitwise_and(k, 0xFFFF) + min16
        lo16[rs, :] = jnp.where(jnp.right_shift(k, 16) == thr_hi, lo, min16).astype(jnp.int16)
        return carry

    lax.fori_loop(0, ntrips, fill_lo, 0)
    thr_lo = search16(lo16, topk - n_above)
    thr = thr_hi * 2 ** 16 + (thr_lo - min16)
    thr_b = bc(thr)
    n_ge = count(lambda k, s: k >= thr_b)
    has_ties = jnp.max(jnp.where(thr > int_min, n_ge, 0)) > topk

    def write(sel):
        def body(c, carry):
            rs = pl.ds(pl.multiple_of(c * CR, CR), CR)
            s_pos = c * CR + row
            keep = sel(sc[rs, :], s_pos) & (s_pos <= t_pos)
            o_ref[0, rs, :] = jnp.where(keep, 0.0, MASKED).astype(o_ref.dtype)
            return carry

        lax.fori_loop(0, ntrips, body, 0)

    @pl.when(jnp.logical_not(has_ties))
    def _():
        write(lambda k, s_pos: k >= thr)

    @pl.when(has_ties)
    def _():
        need = bc(topk - count(lambda k, s: k > thr_b))

        def idx_step(b, lim):
            cand = lim + jnp.left_shift(jnp.int32(1), idx_bits - 1 - b)
            cand_b = bc(cand)
            n_before = count(lambda k, s: (k == thr_b) & (s < cand_b))
            return jnp.where(n_before < need[0, 0:1], cand, lim)

        lim = lax.fori_loop(0, idx_bits, idx_step, jnp.zeros((1, TQ), jnp.int32))
        write(lambda k, s_pos: (k > thr) | ((k == thr) & (s_pos <= lim)))

    def clear(c, carry):
        o_ref[0, pl.ds(pl.multiple_of(c * CR, CR), CR), :] = jnp.full((CR, TQ), MASKED, o_ref.dtype)
        return carry

    lax.fori_loop(ntrips, S // CR, clear, 0)


def _index_mask(qi_g, ki_t, wi, batch, seq, topk, TQ=256, TS=128):
    G, n, gw = qi_g.shape
    DI = ki_t.shape[1]
    HG = gw // DI
    NH = G * HG
    TQ = min(TQ, seq)
    TS = min(TS, TQ)
    nq = seq // TQ
    GW = 2 if nq % 2 == 0 else 1
    idx_bits = max(1, int(seq - 1).bit_length())
    return pl.pallas_call(
        functools.partial(_index_mask_kernel, TQ=TQ, TS=TS, S=seq, G=G, HG=HG, DI=DI, GW=GW, topk=topk,
                          idx_bits=idx_bits),
        grid=(batch, nq),
        in_specs=[
            pl.BlockSpec((G, TQ, gw), lambda b, i: (0, b * nq + i, 0)),
            pl.BlockSpec((1, DI, seq), lambda b, i: (b, 0, 0)),
            pl.BlockSpec((TQ, NH), lambda b, i: (b * nq + i, 0)),
        ],
        out_specs=pl.BlockSpec((1, seq, TQ), lambda b, i: (b, 0, i)),
        out_shape=jax.ShapeDtypeStruct((batch, seq, seq), BF16),
        scratch_shapes=[
            pltpu.VMEM((seq, TQ), jnp.int32),
            pltpu.VMEM((seq, TQ), jnp.int16),
            pltpu.VMEM((seq, TQ), jnp.int16),
            pltpu.VMEM((NH, TQ, TS), F32),
            pltpu.VMEM((GW * TQ // TS, HG * DI, HG * TS), BF16),
        ],
        compiler_params=_params("parallel", "parallel"),
        name="index_mask",
    )(qi_g, ki_t, wi)


def _attn_kernel(ii_ref, jj_ref, qt_ref, k_ref, vt_ref, mask_ref, bias_ref, bfar_ref, o_ref,
                 m_scr, r_scr, a_scr, z_scr, acc_scr, *, H, DH, VA, NEAR, KT, T, c2):
    pair = pl.program_id(1)
    i = ii_ref[pair]
    jg = jj_ref[pair]

    @pl.when(jg == 0)
    def _():
        m_scr[...] = jnp.full_like(m_scr, MASKED)
        acc_scr[...] = jnp.zeros_like(acc_scr)

    def key_tile(sub, carry):
        j = jg * KT + sub
        ks = pl.ds(pl.multiple_of(sub * T, T), T)

        def scores(near):
            mask = mask_ref[0, ks, :].astype(F32)
            for h in range(H):
                hs = slice(h * DH, (h + 1) * DH)
                z = jnp.dot(k_ref[ks, hs], qt_ref[hs, :], preferred_element_type=F32) * c2 + mask
                m_old = m_scr[h]
                if near is not None:
                    z = z + bias_ref[h, near]
                    r = jnp.maximum(m_old, jnp.max(z, axis=0, keepdims=True)).astype(BF16)
                    m_new = r.astype(F32)
                else:
                    shift = bfar_ref[h]
                    r = (jnp.maximum(m_old, jnp.max(z, axis=0, keepdims=True) + shift) - shift).astype(BF16)
                    m_new = r.astype(F32) + shift
                z_scr[h] = z.astype(BF16)
                a_scr[h] = jnp.exp2(m_old - m_new)
                r_scr[h] = jnp.broadcast_to(r, (BF16_ROWS, T))
                m_scr[h] = m_new

        for d in range(NEAR):
            pl.when(i - j == d)(functools.partial(scores, d))
        pl.when(i - j >= NEAR)(functools.partial(scores, None))

        for h in range(H):
            z3 = z_scr[h].reshape(T // BF16_ROWS, BF16_ROWS, T)
            p = jnp.exp2(z3 - r_scr[h][None]).reshape(T, T)
            pv = jnp.dot(vt_ref[h * VA:(h + 1) * VA, ks], p, preferred_element_type=F32)
            acc_scr[h] = a_scr[h] * acc_scr[h] + pv
        return carry

    lax.fori_loop(0, jnp.minimum(KT, i - jg * KT + 1), key_tile, 0)

    @pl.when(jg * KT + KT > i)
    def _():
        for h in range(H):
            a = acc_scr[h]
            o_ref[:, h * DH:(h + 1) * DH] = (a[:DH] / a[DH:DH + 1]).T.astype(o_ref.dtype)


def _attention(q_t, k, vt_aug, mask_t, bias_t, bias_far, batch, seq, n_heads, T):
    width, n = q_t.shape
    DH = width // n_heads
    VA = vt_aug.shape[0] // n_heads
    nt = seq // T
    NEAR = bias_t.shape[1]
    c2 = (DH ** -0.5) * math.log2(math.e)
    KT = next(c for c in (4, 2, 1) if nt % c == 0)
    ng = nt // KT
    pairs = [(i, jg) for i in range(nt) for jg in range(i // KT + 1)]
    ii = jnp.asarray([p[0] for p in pairs], jnp.int32)
    jj = jnp.asarray([p[1] for p in pairs], jnp.int32)
    stat = pltpu.VMEM((n_heads, 1, T), F32)
    grid_spec = pltpu.PrefetchScalarGridSpec(
        num_scalar_prefetch=2,
        grid=(batch, len(pairs)),
        in_specs=[
            pl.BlockSpec((width, T), lambda b, p, ii, jj: (0, b * nt + ii[p])),
            pl.BlockSpec((KT * T, width), lambda b, p, ii, jj: (b * ng + jj[p], 0)),
            pl.BlockSpec((n_heads * VA, KT * T), lambda b, p, ii, jj: (0, b * ng + jj[p])),
            pl.BlockSpec((1, KT * T, T), lambda b, p, ii, jj: (b, jj[p], ii[p])),
            pl.BlockSpec(bias_t.shape, lambda b, p, ii, jj: (0, 0, 0, 0)),
            pl.BlockSpec(bias_far.shape, lambda b, p, ii, jj: (0, 0, 0)),
        ],
        out_specs=pl.BlockSpec((T, width), lambda b, p, ii, jj: (b * nt + ii[p], 0)),
        scratch_shapes=[stat, pltpu.VMEM((n_heads, BF16_ROWS, T), BF16), stat,
                        pltpu.VMEM((n_heads, T, T), BF16), pltpu.VMEM((n_heads, VA, T), F32)],
    )
    return pl.pallas_call(
        functools.partial(_attn_kernel, H=n_heads, DH=DH, VA=VA, NEAR=NEAR, KT=KT, T=T, c2=c2),
        grid_spec=grid_spec,
        out_shape=jax.ShapeDtypeStruct((n, width), BF16),
        compiler_params=_params("parallel", "arbitrary"),
        name="masked_attention",
    )(ii, jj, q_t, k, vt_aug, mask_t, bias_t, bias_far)


def _t5_bucket(n):
    max_exact = N_BUCKETS // 2
    nf = jnp.maximum(n, 1).astype(F32)
    large = max_exact + (jnp.log(nf / max_exact) / math.log(MAX_DISTANCE / max_exact)
                         * (N_BUCKETS - max_exact)).astype(jnp.int32)
    large = jnp.minimum(large, N_BUCKETS - 1)
    return jnp.where(n < max_exact, n, large)


def _bias_tiles(rel_bias, T, near):
    s = jnp.arange(T, dtype=jnp.int32)[:, None]
    t = jnp.arange(T, dtype=jnp.int32)[None, :]
    rb = rel_bias.astype(F32) * math.log2(math.e)
    tiles = []
    for d in range(near):
        bucket = _t5_bucket(jnp.maximum(d * T + t - s, 0))
        onehot = bucket[None, :, :] == jnp.arange(N_BUCKETS, dtype=jnp.int32)[:, None, None]
        tiles.append(jnp.sum(jnp.where(onehot[:, None], rb[:, :, None, None], 0.0), axis=0))
    far = jnp.broadcast_to(rb[N_BUCKETS - 1][:, None, None], (rb.shape[1], 1, T))
    return jnp.stack(tiles, axis=1), far


def kernel(x, norm1_g, w_in, b_gate, conv_w, conv_bias, conv_ln_g, conv_ln_b, w_conv_out, w_attn_out,
           rel_bias, w_o, norm2_g, w_ff1, w_ff2, normf_g):
    B, S, D = x.shape
    N = B * S
    depth = w_in.shape[0]
    C = conv_w.shape[-1]
    A = w_attn_out.shape[1]
    H = rel_bias.shape[1]
    DH = A // H
    NH, DI = IDX_HEADS, IDX_DIM
    HG = MXU_DIM // DI
    topk = min(TOPK_MAX, S // 4)
    off_q = 2 * C
    off_iq = off_q + 3 * A
    off_ik = off_iq + NH * DI
    off_iw = off_ik + DI
    off_gate = off_iw + NH
    T_ATT = min(256, S)
    near = -(-(MAX_DISTANCE + T_ATT - 1) // T_ATT)
    bias_t, bias_far = _bias_tiles(rel_bias, T_ATT, near)
    sigmoid_bias = lambda acc, b: jax.nn.sigmoid(acc + b)
    relu2 = lambda acc: jnp.square(jnp.maximum(acc, 0.0))
    add = lambda acc, r: acc + r

    h = x.reshape(N, D)
    for l in range(depth):
        wl = w_in[l]
        u = _rmsnorm(h, norm1_g[l], BF16)
        wb16 = wl.astype(BF16)
        p_conv = _matmul(u, wb16, F32, wcols=(0, off_q), name="proj_conv")
        q_t = _matmul_t(u, wb16, BF16, wcols=(off_q, A), name="proj_q")
        k = _matmul(u, wb16, BF16, wcols=(off_q + A, A), name="proj_k")
        vt_aug = _matmul_t(u, wb16, BF16, wcols=(off_q + 2 * A, A), DH=DH, ones_rows=BF16_ROWS, name="proj_v")
        qi_g = _matmul(u, wb16, BF16, wcols=(off_iq, NH * DI), group_width=HG * DI, name="proj_qi")
        kw = _matmul(u, wb16, F32, wcols=(off_ik, LANES), name="proj_ki_wi")
        gates = _matmul(u, wb16[:, off_gate:], BF16, epilogue=sigmoid_bias,
                        extras=[("row", b_gate[l].reshape(1, -1))], name="proj_gates")

        cf = _conv_branch(p_conv, conv_w[l], conv_bias[l], conv_ln_g[l], conv_ln_b[l], B, S)

        ki_t = jnp.transpose(kw[:, :DI].astype(BF16).reshape(B, S, DI), (0, 2, 1))
        wi = kw[:, DI:DI + NH] * (NH ** -0.5) * (DI ** -0.5)
        mask_t = _index_mask(qi_g, ki_t, wi, B, S, topk)
        o = _attention(q_t, k, vt_aug, mask_t, bias_t, bias_far, B, S, H, T_ATT)

        mixed = _gated_mix(cf, w_conv_out[l].astype(BF16), o, w_attn_out[l].astype(BF16), gates, BF16)
        h = _matmul(mixed, w_o[l].astype(BF16), F32, epilogue=add, extras=[("tile", h, 0)], tn=512, name="w_o")
        u2 = _rmsnorm(h, norm2_g[l], BF16)
        act = _matmul(u2, w_ff1[l], BF16, epilogue=relu2, name="ff1")
        h = _matmul(act, w_ff2[l].astype(BF16), F32, epilogue=add, extras=[("tile", h, 0)], tn=512, name="ff2")
    return _rmsnorm(h, normf_g, F32).reshape(B, S, D)
```

```python
import functools
import math

import jax
import jax.numpy as jnp
from jax import lax
from jax.experimental import pallas as pl
from jax.experimental.pallas import tpu as pltpu

IDX_HEADS = 32
IDX_DIM = 64
TOPK_MAX = 256
N_BUCKETS = 32
MAX_DISTANCE = 128
EPS = 1e-6

V7X_VMEM_BYTES = 64 * 1024 * 1024
VMEM_LIMIT_BYTES = V7X_VMEM_BYTES - 2 * 1024 * 1024
LANES = 128
SUBLANES = 8
BF16_ROWS = 16
MXU_DIM = 256
FF2_TK = 8192
MASKED = -(2.0 ** 100)

F32 = jnp.float32
BF16 = jnp.bfloat16


def _params(*sem):
    return pltpu.CompilerParams(dimension_semantics=sem, vmem_limit_bytes=VMEM_LIMIT_BYTES)


def _rmsnorm_kernel(x_ref, g_ref, o_ref):
    x = x_ref[...]
    ms = jnp.mean(x * x, axis=-1, keepdims=True)
    o_ref[...] = (x * lax.rsqrt(ms + EPS) * g_ref[...]).astype(o_ref.dtype)


def _rmsnorm(x, g, out_dtype, tr=256):
    n, d = x.shape
    tr = min(tr, n)
    return pl.pallas_call(
        _rmsnorm_kernel,
        grid=(n // tr,),
        in_specs=[pl.BlockSpec((tr, d), lambda i: (i, 0)), pl.BlockSpec((1, d), lambda i: (0, 0))],
        out_specs=pl.BlockSpec((tr, d), lambda i: (i, 0)),
        out_shape=jax.ShapeDtypeStruct((n, d), out_dtype),
        compiler_params=_params("parallel"),
        name="rmsnorm",
    )(x, g.reshape(1, d))


def _mm_kernel(*refs, nk, n_extra, epilogue, group_width):
    a_ref, w_ref = refs[0], refs[1]
    extra = refs[2:2 + n_extra]
    o_ref = refs[2 + n_extra]

    def emit(acc):
        res = epilogue(acc, *[e[...] for e in extra]).astype(o_ref.dtype)
        if group_width:
            for g in range(res.shape[1] // group_width):
                o_ref[g] = res[:, g * group_width:(g + 1) * group_width]
        else:
            o_ref[...] = res

    if nk == 1:
        emit(jnp.dot(a_ref[...], w_ref[...].astype(BF16), preferred_element_type=F32))
        return
    acc_ref = refs[3 + n_extra]
    k = pl.program_id(2)

    @pl.when(k == 0)
    def _():
        acc_ref[...] = jnp.zeros_like(acc_ref)

    acc_ref[...] += jnp.dot(a_ref[...], w_ref[...].astype(BF16), preferred_element_type=F32)

    @pl.when(k == nk - 1)
    def _():
        emit(acc_ref[...])


def _wcols(w, wcols, tn):
    col0, n = wcols if wcols else (0, w.shape[1])
    tn = min(tn, n)
    assert col0 % tn == 0 and n % tn == 0
    return col0 // tn, n, tn


def _matmul(a, w, out_dtype, *, epilogue=None, extras=(), tm=1024, tn=1024, tk=4096, group_width=0, wcols=None,
            name="matmul"):
    m, kd = a.shape
    jb0, n, tn = _wcols(w, wcols, tn)
    tm, tk = min(tm, m), min(tk, kd)
    nk = kd // tk
    if epilogue is None:
        epilogue = lambda acc: acc
    in_specs = [pl.BlockSpec((tm, tk), lambda i, j, k: (i, k)), pl.BlockSpec((tk, tn), lambda i, j, k: (k, j + jb0))]
    args = [a, w]
    for ex in extras:
        if ex[0] == "row":
            in_specs.append(pl.BlockSpec((1, tn), lambda i, j, k: (0, j)))
        else:
            off = ex[2] // tn
            in_specs.append(pl.BlockSpec((tm, tn), lambda i, j, k, off=off: (i, j + off)))
        args.append(ex[1])
    if group_width:
        gpt = tn // group_width
        out_specs = pl.BlockSpec((gpt, tm, group_width), lambda i, j, k: (j, i, 0))
        out_shape = jax.ShapeDtypeStruct((n // group_width, m, group_width), out_dtype)
    else:
        out_specs = pl.BlockSpec((tm, tn), lambda i, j, k: (i, j))
        out_shape = jax.ShapeDtypeStruct((m, n), out_dtype)
    return pl.pallas_call(
        functools.partial(_mm_kernel, nk=nk, n_extra=len(extras), epilogue=epilogue, group_width=group_width),
        grid=(m // tm, n // tn, nk),
        in_specs=in_specs,
        out_specs=out_specs,
        out_shape=out_shape,
        scratch_shapes=[pltpu.VMEM((tm, tn), F32)] if nk > 1 else [],
        compiler_params=_params("parallel", "parallel", "arbitrary"),
        name=name,
    )(*args)


def _mm_t_kernel(a_ref, w_ref, o_ref, *, DH, ones_rows):
    res = jnp.dot(a_ref[...], w_ref[...].astype(BF16), preferred_element_type=F32).T.astype(o_ref.dtype)
    if not ones_rows:
        o_ref[...] = res
        return
    VA = DH + ones_rows
    for h in range(res.shape[0] // DH):
        o_ref[h * VA:h * VA + DH, :] = res[h * DH:(h + 1) * DH, :]
        o_ref[h * VA + DH:(h + 1) * VA, :] = jnp.ones((ones_rows, res.shape[1]), o_ref.dtype)


def _matmul_t(a, w, out_dtype, *, tm=1024, tn=1024, DH=0, ones_rows=0, wcols=None, name="matmul_t"):
    m, kd = a.shape
    jb0, n, tn = _wcols(w, wcols, tn)
    tm = min(tm, m)
    rows = tn // DH * (DH + ones_rows) if ones_rows else tn
    return pl.pallas_call(
        functools.partial(_mm_t_kernel, DH=DH, ones_rows=ones_rows),
        grid=(m // tm, n // tn),
        in_specs=[pl.BlockSpec((tm, kd), lambda i, j: (i, 0)), pl.BlockSpec((kd, tn), lambda i, j: (0, j + jb0))],
        out_specs=pl.BlockSpec((rows, tm), lambda i, j: (j, i)),
        out_shape=jax.ShapeDtypeStruct((n // tn * rows, m), out_dtype),
        compiler_params=_params("parallel", "parallel"),
        name=name,
    )(a, w)


def _mm_kouter_kernel(a_ref, w_ref, r_ref, o_ref, acc_ref, *, nk, tn):
    k = pl.program_id(1)
    cs = pl.ds(pl.multiple_of(pl.program_id(2) * tn, tn), tn)
    part = jnp.dot(a_ref[...], w_ref[...].astype(BF16), preferred_element_type=F32)

    @pl.when(k == 0)
    def _():
        acc_ref[:, cs] = part

    @pl.when(jnp.logical_and(k > 0, k < nk - 1))
    def _():
        acc_ref[:, cs] += part

    @pl.when(k == nk - 1)
    def _():
        o_ref[...] = (acc_ref[:, cs] + part + r_ref[...]).astype(o_ref.dtype)


def _matmul_residual_kouter(a, w, res, out_dtype, *, tm=1024, tn=512, tk=4096, name="matmul_kouter"):
    m, kd = a.shape
    n = w.shape[1]
    tm, tn, tk = min(tm, m), min(tn, n), min(tk, kd)
    nk = kd // tk
    if nk < 2:
        return _matmul(a, w, out_dtype, epilogue=lambda acc, r: acc + r, extras=[("tile", res, 0)], tm=tm, tn=tn,
                       name=name)
    last_j = lambda i, k, j: (i, jnp.where(k == nk - 1, j, 0))
    return pl.pallas_call(
        functools.partial(_mm_kouter_kernel, nk=nk, tn=tn),
        grid=(m // tm, nk, n // tn),
        in_specs=[
            pl.BlockSpec((tm, tk), lambda i, k, j: (i, k)),
            pl.BlockSpec((tk, tn), lambda i, k, j: (k, j)),
            pl.BlockSpec((tm, tn), last_j),
        ],
        out_specs=pl.BlockSpec((tm, tn), last_j),
        out_shape=jax.ShapeDtypeStruct((m, n), out_dtype),
        scratch_shapes=[pltpu.VMEM((tm, n), F32)],
        compiler_params=_params("parallel", "arbitrary", "arbitrary"),
        name=name,
    )(a, w, res)


def _mix_kernel(a1_ref, w1_ref, a2_ref, w2_ref, g1_ref, g2_ref, o_ref):
    y1 = jnp.dot(a1_ref[...], w1_ref[...].astype(BF16), preferred_element_type=F32)
    y2 = jnp.dot(a2_ref[...], w2_ref[...].astype(BF16), preferred_element_type=F32)
    o_ref[...] = (g1_ref[...].astype(F32) * y1 + g2_ref[...].astype(F32) * y2).astype(o_ref.dtype)


def _gated_mix(a1, w1, a2, w2, gates, out_dtype, tm=1024, tn=1024):
    m, k1 = a1.shape
    k2 = a2.shape[1]
    n = w1.shape[1]
    tm, tn = min(tm, m), min(tn, n)
    nb = n // tn
    return pl.pallas_call(
        _mix_kernel,
        grid=(m // tm, nb),
        in_specs=[
            pl.BlockSpec((tm, k1), lambda i, j: (i, 0)),
            pl.BlockSpec((k1, tn), lambda i, j: (0, j)),
            pl.BlockSpec((tm, k2), lambda i, j: (i, 0)),
            pl.BlockSpec((k2, tn), lambda i, j: (0, j)),
            pl.BlockSpec((tm, tn), lambda i, j: (i, j)),
            pl.BlockSpec((tm, tn), lambda i, j: (i, j + nb)),
        ],
        out_specs=pl.BlockSpec((tm, tn), lambda i, j: (i, j)),
        out_shape=jax.ShapeDtypeStruct((m, n), out_dtype),
        compiler_params=_params("parallel", "parallel"),
        name="gated_mix",
    )(a1, w1, a2, w2, gates, gates)


def _conv_kernel(a_ref, g_ref, ah_ref, gh_ref, w_ref, cb_ref, lng_ref, lnb_ref, o_ref, hbuf, cbuf, sh,
                 *, T, C, KW, HALO, RC):
    i = pl.program_id(1)
    hh = ah_ref[...] * jax.nn.sigmoid(gh_ref[...])
    hbuf[0:HALO, :] = jnp.where(i > 0, hh, 0.0)
    hbuf[HALO:HALO + T, :] = a_ref[...] * jax.nn.sigmoid(g_ref[...])
    base = HALO - (KW - 1)
    NR = T + HALO - SUBLANES

    def chunk(c, carry):
        cs = pl.ds(pl.multiple_of(c * LANES, LANES), LANES)
        for b in range(1, SUBLANES):
            sh[b - 1, 0:NR, :] = hbuf[b:b + NR, cs]
        for r in range(T // RC):
            acc = jnp.zeros((RC, LANES), F32)
            for j in range(KW):
                a8, b = divmod(base + j, SUBLANES)
                r0 = r * RC + a8 * SUBLANES
                src = hbuf[r0:r0 + RC, cs] if b == 0 else sh[b - 1, r0:r0 + RC, :]
                acc = acc + w_ref[j:j + 1, cs] * src
            cbuf[r * RC:(r + 1) * RC, cs] = acc + cb_ref[:, cs]
        return carry

    lax.fori_loop(0, C // LANES, chunk, 0)

    RN = min(64, T)

    def norm(r, carry):
        rs = pl.ds(pl.multiple_of(r * RN, RN), RN)
        x = cbuf[rs, :]
        mu = jnp.mean(x, axis=-1, keepdims=True)
        xc = x - mu
        var = jnp.mean(xc * xc, axis=-1, keepdims=True)
        y = xc * lax.rsqrt(var + EPS) * lng_ref[...] + lnb_ref[...]
        o_ref[rs, :] = (y * jax.nn.sigmoid(y)).astype(o_ref.dtype)
        return carry

    lax.fori_loop(0, T // RN, norm, 0)


def _conv_branch(p_conv, conv_w, conv_bias, ln_g, ln_b, batch, seq, T=256):
    n, c2 = p_conv.shape
    C = c2 // 2
    KW = conv_w.shape[0]
    HALO = 32
    assert KW - 1 <= HALO
    T = min(T, seq)
    RC = min(64, T)
    nt = seq // T
    hb = T // HALO
    row = lambda b, i: b * nt + i
    halo_row = lambda b, i: jnp.maximum((b * nt + i) * hb - 1, 0)
    vec = lambda v: v.reshape(1, C)
    return pl.pallas_call(
        functools.partial(_conv_kernel, T=T, C=C, KW=KW, HALO=HALO, RC=RC),
        grid=(batch, nt),
        in_specs=[
            pl.BlockSpec((T, C), lambda b, i: (row(b, i), 0)),
            pl.BlockSpec((T, C), lambda b, i: (row(b, i), 1)),
            pl.BlockSpec((HALO, C), lambda b, i: (halo_row(b, i), 0)),
            pl.BlockSpec((HALO, C), lambda b, i: (halo_row(b, i), 1)),
            pl.BlockSpec((KW, C), lambda b, i: (0, 0)),
            pl.BlockSpec((1, C), lambda b, i: (0, 0)),
            pl.BlockSpec((1, C), lambda b, i: (0, 0)),
            pl.BlockSpec((1, C), lambda b, i: (0, 0)),
        ],
        out_specs=pl.BlockSpec((T, C), lambda b, i: (row(b, i), 0)),
        out_shape=jax.ShapeDtypeStruct((n, C), BF16),
        scratch_shapes=[pltpu.VMEM((HALO + T, C), F32), pltpu.VMEM((T, C), F32),
                        pltpu.VMEM((SUBLANES - 1, HALO + T, LANES), F32)],
        compiler_params=_params("parallel", "parallel"),
        name="conv_branch",
    )(p_conv, p_conv, p_conv, p_conv, conv_w, vec(conv_bias), vec(ln_g), vec(ln_b))


def _index_mask_kernel(q_ref, kt_ref, w_ref, o_ref, sc, hi16, lo16, wb, rhs,
                       *, TQ, TS, S, G, HG, DI, GW, topk, idx_bits):
    i = pl.program_id(1)
    NH = G * HG
    CR = GW * TQ
    ntrips = (i + GW) // GW
    NACC = 4
    int_min = jnp.int32(-2 ** 31)

    for h in range(NH):
        wb[h] = jnp.broadcast_to(w_ref[:, h:h + 1], (TQ, TS))
    rhs[...] = jnp.zeros_like(rhs)
    q = q_ref[...].reshape(G * TQ, HG * DI)
    t_pos_c = i * TQ + lax.broadcasted_iota(jnp.int32, (TS, TQ), 1)
    row_c = lax.broadcasted_iota(jnp.int32, (TS, TQ), 0)

    def score_chunk(c, slot):
        cs = pl.ds(pl.multiple_of(c * TS, TS), TS)
        kt = kt_ref[0, :, cs]
        for h in range(HG):
            rhs[slot, h * DI:(h + 1) * DI, h * TS:(h + 1) * TS] = kt
        x = jnp.dot(q, rhs[slot], preferred_element_type=F32)
        acc = jnp.zeros((TQ, TS), F32)
        for g in range(G):
            for h in range(HG):
                acc = acc + wb[g * HG + h] * jnp.maximum(x[g * TQ:(g + 1) * TQ, h * TS:(h + 1) * TS], 0.0)
        acc = acc.T + 0.0
        bits = pltpu.bitcast(acc, jnp.int32)
        key = jnp.where(bits < 0, bits ^ jnp.int32(0x7FFFFFFF), bits)
        key = jnp.where(c * TS + row_c <= t_pos_c, key, int_min)
        sc[cs, :] = key
        hi16[cs, :] = jnp.right_shift(key, 16).astype(jnp.int16)

    CPT = CR // TS

    def score_group(cg, carry):
        for u in range(CPT):
            score_chunk(cg * CPT + u, u)
        return carry

    lax.fori_loop(0, ntrips, score_group, 0)

    t_pos = i * TQ + lax.broadcasted_iota(jnp.int32, (CR, TQ), 1)
    row = lax.broadcasted_iota(jnp.int32, (CR, TQ), 0)
    row3 = row.reshape(CR // SUBLANES, SUBLANES, TQ)

    def count(pred):
        def body(c, cnt):
            k3 = sc[pl.ds(pl.multiple_of(c * CR, CR), CR), :].reshape(CR // SUBLANES, SUBLANES, TQ)
            hit = pred(k3, c * CR + row3).astype(jnp.int32)
            return cnt + jnp.sum(hit.reshape(CR // (SUBLANES * NACC), NACC, SUBLANES, TQ), axis=0)

        cnt = lax.fori_loop(0, ntrips, body, jnp.zeros((NACC, SUBLANES, TQ), jnp.int32))
        return jnp.sum(cnt.reshape(NACC * SUBLANES, TQ), axis=0, keepdims=True)

    bc = lambda v: jnp.broadcast_to(v, (SUBLANES, TQ))[None]
    bc16 = lambda v: jnp.broadcast_to(v, (BF16_ROWS, TQ)).astype(jnp.int16)[None]
    G16 = CR // (BF16_ROWS * NACC)
    assert (S // CR) * G16 < 2 ** 15
    min16 = -2 ** 15

    def count16(ref, cand):
        cand_b = bc16(cand)

        def body(c, cnt):
            k3 = ref[pl.ds(pl.multiple_of(c * CR, CR), CR), :].reshape(CR // BF16_ROWS, BF16_ROWS, TQ)
            hit = jnp.where(k3 >= cand_b, jnp.int16(1), jnp.int16(0)).reshape(G16, NACC, BF16_ROWS, TQ)
            for g in range(G16):
                cnt = cnt + hit[g]
            return cnt

        cnt = lax.fori_loop(0, ntrips, body, jnp.zeros((NACC, BF16_ROWS, TQ), jnp.int16))
        return jnp.sum(cnt.astype(jnp.int32).reshape(NACC * BF16_ROWS, TQ), axis=0, keepdims=True)

    def search16(ref, need):
        def bit_step(b, cur):
            cand = cur + jnp.left_shift(jnp.int32(1), 15 - b)
            return jnp.where(count16(ref, cand) >= need, cand, cur)

        return lax.fori_loop(0, 16, bit_step, jnp.full((1, TQ), min16, jnp.int32))

    thr_hi = search16(hi16, topk)
    n_above = jnp.where(thr_hi == 2 ** 15 - 1, 0, count16(hi16, jnp.minimum(thr_hi + 1, 2 ** 15 - 1)))

    def fill_lo(c, carry):
        rs = pl.ds(pl.multiple_of(c * CR, CR), CR)
        k = sc[rs, :]
        lo = jnp.bitwise_and(k, 0xFFFF) + min16
        lo16[rs, :] = jnp.where(jnp.right_shift(k, 16) == thr_hi, lo, min16).astype(jnp.int16)
        return carry

    lax.fori_loop(0, ntrips, fill_lo, 0)
    thr_lo = search16(lo16, topk - n_above)
    thr = thr_hi * 2 ** 16 + (thr_lo - min16)
    thr_b = bc(thr)
    n_ge = count(lambda k, s: k >= thr_b)
    has_ties = jnp.max(jnp.where(thr > int_min, n_ge, 0)) > topk

    def write(sel):
        def body(c, carry):
            rs = pl.ds(pl.multiple_of(c * CR, CR), CR)
            s_pos = c * CR + row
            keep = sel(sc[rs, :], s_pos) & (s_pos <= t_pos)
            o_ref[0, rs, :] = jnp.where(keep, 0.0, MASKED).astype(o_ref.dtype)
            return carry

        lax.fori_loop(0, ntrips, body, 0)

    @pl.when(jnp.logical_not(has_ties))
    def _():
        write(lambda k, s_pos: k >= thr)

    @pl.when(has_ties)
    def _():
        need = bc(topk - count(lambda k, s: k > thr_b))

        def idx_step(b, lim):
            cand = lim + jnp.left_shift(jnp.int32(1), idx_bits - 1 - b)
            cand_b = bc(cand)
            n_before = count(lambda k, s: (k == thr_b) & (s < cand_b))
            return jnp.where(n_before < need[0, 0:1], cand, lim)

        lim = lax.fori_loop(0, idx_bits, idx_step, jnp.zeros((1, TQ), jnp.int32))
        write(lambda k, s_pos: (k > thr) | ((k == thr) & (s_pos <= lim)))

    def clear(c, carry):
        o_ref[0, pl.ds(pl.multiple_of(c * CR, CR), CR), :] = jnp.full((CR, TQ), MASKED, o_ref.dtype)
        return carry

    lax.fori_loop(ntrips, S // CR, clear, 0)


def _index_mask(qi_g, ki_t, wi, batch, seq, topk, TQ=256, TS=128):
    G, n, gw = qi_g.shape
    DI = ki_t.shape[1]
    HG = gw // DI
    NH = G * HG
    TQ = min(TQ, seq)
    TS = min(TS, TQ)
    nq = seq // TQ
    GW = 2 if nq % 2 == 0 else 1
    idx_bits = max(1, int(seq - 1).bit_length())
    return pl.pallas_call(
        functools.partial(_index_mask_kernel, TQ=TQ, TS=TS, S=seq, G=G, HG=HG, DI=DI, GW=GW, topk=topk,
                          idx_bits=idx_bits),
        grid=(batch, nq),
        in_specs=[
            pl.BlockSpec((G, TQ, gw), lambda b, i: (0, b * nq + i, 0)),
            pl.BlockSpec((1, DI, seq), lambda b, i: (b, 0, 0)),
            pl.BlockSpec((TQ, NH), lambda b, i: (b * nq + i, 0)),
        ],
        out_specs=pl.BlockSpec((1, seq, TQ), lambda b, i: (b, 0, i)),
        out_shape=jax.ShapeDtypeStruct((batch, seq, seq), BF16),
        scratch_shapes=[
            pltpu.VMEM((seq, TQ), jnp.int32),
            pltpu.VMEM((seq, TQ), jnp.int16),
            pltpu.VMEM((seq, TQ), jnp.int16),
            pltpu.VMEM((NH, TQ, TS), F32),
            pltpu.VMEM((GW * TQ // TS, HG * DI, HG * TS), BF16),
        ],
        compiler_params=_params("parallel", "parallel"),
        name="index_mask",
    )(qi_g, ki_t, wi)


def _attn_kernel(ii_ref, jj_ref, qt_ref, k_ref, vt_ref, mask_ref, bias_ref, bfar_ref, o_ref,
                 m_scr, r_scr, a_scr, z_scr, acc_scr, *, H, DH, VA, NEAR, KT, T, c2):
    pair = pl.program_id(1)
    i = ii_ref[pair]
    jg = jj_ref[pair]

    @pl.when(jg == 0)
    def _():
        m_scr[...] = jnp.full_like(m_scr, MASKED)
        acc_scr[...] = jnp.zeros_like(acc_scr)

    def key_tile(sub, carry):
        j = jg * KT + sub
        ks = pl.ds(pl.multiple_of(sub * T, T), T)

        def scores(near):
            mask = mask_ref[0, ks, :].astype(F32)
            for h in range(H):
                hs = slice(h * DH, (h + 1) * DH)
                z = jnp.dot(k_ref[ks, hs], qt_ref[hs, :], preferred_element_type=F32) * c2 + mask
                m_old = m_scr[h]
                if near is not None:
                    z = z + bias_ref[h, near]
                    r = jnp.maximum(m_old, jnp.max(z, axis=0, keepdims=True)).astype(BF16)
                    m_new = r.astype(F32)
                else:
                    shift = bfar_ref[h]
                    r = (jnp.maximum(m_old, jnp.max(z, axis=0, keepdims=True) + shift) - shift).astype(BF16)
                    m_new = r.astype(F32) + shift
                z_scr[h] = z.astype(BF16)
                a_scr[h] = jnp.exp2(m_old - m_new)
                r_scr[h] = jnp.broadcast_to(r, (BF16_ROWS, T))
                m_scr[h] = m_new

        for d in range(NEAR):
            pl.when(i - j == d)(functools.partial(scores, d))
        pl.when(i - j >= NEAR)(functools.partial(scores, None))

        for h in range(H):
            z3 = z_scr[h].reshape(T // BF16_ROWS, BF16_ROWS, T)
            p = jnp.exp2(z3 - r_scr[h][None]).reshape(T, T)
            pv = jnp.dot(vt_ref[h * VA:(h + 1) * VA, ks], p, preferred_element_type=F32)
            acc_scr[h] = a_scr[h] * acc_scr[h] + pv
        return carry

    lax.fori_loop(0, jnp.minimum(KT, i - jg * KT + 1), key_tile, 0)

    @pl.when(jg * KT + KT > i)
    def _():
        for h in range(H):
            a = acc_scr[h]
            o_ref[:, h * DH:(h + 1) * DH] = (a[:DH] / a[DH:DH + 1]).T.astype(o_ref.dtype)


def _attention(q_t, k, vt_aug, mask_t, bias_t, bias_far, batch, seq, n_heads, T):
    width, n = q_t.shape
    DH = width // n_heads
    VA = vt_aug.shape[0] // n_heads
    nt = seq // T
    NEAR = bias_t.shape[1]
    c2 = (DH ** -0.5) * math.log2(math.e)
    KT = next(c for c in (4, 2, 1) if nt % c == 0)
    ng = nt // KT
    pairs = [(i, jg) for i in range(nt) for jg in range(i // KT + 1)]
    ii = jnp.asarray([p[0] for p in pairs], jnp.int32)
    jj = jnp.asarray([p[1] for p in pairs], jnp.int32)
    stat = pltpu.VMEM((n_heads, 1, T), F32)
    grid_spec = pltpu.PrefetchScalarGridSpec(
        num_scalar_prefetch=2,
        grid=(batch, len(pairs)),
        in_specs=[
            pl.BlockSpec((width, T), lambda b, p, ii, jj: (0, b * nt + ii[p])),
            pl.BlockSpec((KT * T, width), lambda b, p, ii, jj: (b * ng + jj[p], 0)),
            pl.BlockSpec((n_heads * VA, KT * T), lambda b, p, ii, jj: (0, b * ng + jj[p])),
            pl.BlockSpec((1, KT * T, T), lambda b, p, ii, jj: (b, jj[p], ii[p])),
            pl.BlockSpec(bias_t.shape, lambda b, p, ii, jj: (0, 0, 0, 0)),
            pl.BlockSpec(bias_far.shape, lambda b, p, ii, jj: (0, 0, 0)),
        ],
        out_specs=pl.BlockSpec((T, width), lambda b, p, ii, jj: (b * nt + ii[p], 0)),
        scratch_shapes=[stat, pltpu.VMEM((n_heads, BF16_ROWS, T), BF16), stat,
                        pltpu.VMEM((n_heads, T, T), BF16), pltpu.VMEM((n_heads, VA, T), F32)],
    )
    return pl.pallas_call(
        functools.partial(_attn_kernel, H=n_heads, DH=DH, VA=VA, NEAR=NEAR, KT=KT, T=T, c2=c2),
        grid_spec=grid_spec,
        out_shape=jax.ShapeDtypeStruct((n, width), BF16),
        compiler_params=_params("parallel", "arbitrary"),
        name="masked_attention",
    )(ii, jj, q_t, k, vt_aug, mask_t, bias_t, bias_far)


def _t5_bucket(n):
    max_exact = N_BUCKETS // 2
    nf = jnp.maximum(n, 1).astype(F32)
    large = max_exact + (jnp.log(nf / max_exact) / math.log(MAX_DISTANCE / max_exact)
                         * (N_BUCKETS - max_exact)).astype(jnp.int32)
    large = jnp.minimum(large, N_BUCKETS - 1)
    return jnp.where(n < max_exact, n, large)


def _bias_tiles(rel_bias, T, near):
    s = jnp.arange(T, dtype=jnp.int32)[:, None]
    t = jnp.arange(T, dtype=jnp.int32)[None, :]
    rb = rel_bias.astype(F32) * math.log2(math.e)
    tiles = []
    for d in range(near):
        bucket = _t5_bucket(jnp.maximum(d * T + t - s, 0))
        onehot = bucket[None, :, :] == jnp.arange(N_BUCKETS, dtype=jnp.int32)[:, None, None]
        tiles.append(jnp.sum(jnp.where(onehot[:, None], rb[:, :, None, None], 0.0), axis=0))
    far = jnp.broadcast_to(rb[N_BUCKETS - 1][:, None, None], (rb.shape[1], 1, T))
    return jnp.stack(tiles, axis=1), far


def kernel(x, norm1_g, w_in, b_gate, conv_w, conv_bias, conv_ln_g, conv_ln_b, w_conv_out, w_attn_out,
           rel_bias, w_o, norm2_g, w_ff1, w_ff2, normf_g):
    B, S, D = x.shape
    N = B * S
    depth = w_in.shape[0]
    C = conv_w.shape[-1]
    A = w_attn_out.shape[1]
    H = rel_bias.shape[1]
    DH = A // H
    NH, DI = IDX_HEADS, IDX_DIM
    HG = MXU_DIM // DI
    topk = min(TOPK_MAX, S // 4)
    off_q = 2 * C
    off_iq = off_q + 3 * A
    off_ik = off_iq + NH * DI
    off_iw = off_ik + DI
    off_gate = off_iw + NH
    T_ATT = min(256, S)
    near = -(-(MAX_DISTANCE + T_ATT - 1) // T_ATT)
    bias_t, bias_far = _bias_tiles(rel_bias, T_ATT, near)
    sigmoid_bias = lambda acc, b: jax.nn.sigmoid(acc + b)
    relu2 = lambda acc: jnp.square(jnp.maximum(acc, 0.0))
    add = lambda acc, r: acc + r

    h = x.reshape(N, D)
    for l in range(depth):
        wl = w_in[l]
        u = _rmsnorm(h, norm1_g[l], BF16)
        wb16 = wl.astype(BF16)
        p_conv = _matmul(u, wb16, F32, wcols=(0, off_q), name="proj_conv")
        q_t = _matmul_t(u, wb16, BF16, wcols=(off_q, A), name="proj_q")
        k = _matmul(u, wb16, BF16, wcols=(off_q + A, A), name="proj_k")
        vt_aug = _matmul_t(u, wb16, BF16, wcols=(off_q + 2 * A, A), DH=DH, ones_rows=BF16_ROWS, name="proj_v")
        qi_g = _matmul(u, wb16, BF16, wcols=(off_iq, NH * DI), group_width=HG * DI, name="proj_qi")
        kw = _matmul(u, wb16, F32, wcols=(off_ik, LANES), name="proj_ki_wi")
        gates = _matmul(u, wb16[:, off_gate:], BF16, epilogue=sigmoid_bias,
                        extras=[("row", b_gate[l].reshape(1, -1))], name="proj_gates")

        cf = _conv_branch(p_conv, conv_w[l], conv_bias[l], conv_ln_g[l], conv_ln_b[l], B, S)

        ki_t = jnp.transpose(kw[:, :DI].astype(BF16).reshape(B, S, DI), (0, 2, 1))
        wi = kw[:, DI:DI + NH] * (NH ** -0.5) * (DI ** -0.5)
        mask_t = _index_mask(qi_g, ki_t, wi, B, S, topk)
        o = _attention(q_t, k, vt_aug, mask_t, bias_t, bias_far, B, S, H, T_ATT)

        mixed = _gated_mix(cf, w_conv_out[l].astype(BF16), o, w_attn_out[l].astype(BF16), gates, BF16)
        h = _matmul(mixed, w_o[l].astype(BF16), F32, epilogue=add, extras=[("tile", h, 0)], name="w_o")
        u2 = _rmsnorm(h, norm2_g[l], BF16)
        act = _matmul(u2, w_ff1[l], BF16, epilogue=relu2, name="ff1")
        h = _matmul_residual_kouter(act, w_ff2[l].astype(BF16), h, F32, tm=512, tk=FF2_TK, name="ff2")
    return _rmsnorm(h, normf_g, F32).reshape(B, S, D)
```

```python
import functools
import math

import jax
import jax.numpy as jnp
from jax import lax
from jax.experimental import pallas as pl
from jax.experimental.pallas import tpu as pltpu

IDX_HEADS = 32
IDX_DIM = 64
TOPK_MAX = 256
N_BUCKETS = 32
MAX_DISTANCE = 128
EPS = 1e-6

V7X_VMEM_BYTES = 64 * 1024 * 1024
VMEM_LIMIT_BYTES = V7X_VMEM_BYTES - 2 * 1024 * 1024
LANES = 128
SUBLANES = 8
BF16_ROWS = 16
MXU_DIM = 256
FF2_TK = 8192
MASKED = -(2.0 ** 100)

F32 = jnp.float32
BF16 = jnp.bfloat16


def _params(*sem):
    return pltpu.CompilerParams(dimension_semantics=sem, vmem_limit_bytes=VMEM_LIMIT_BYTES)


def _sigmoid(x):
    return 0.5 * jnp.tanh(0.5 * x) + 0.5


def _rmsnorm_kernel(x_ref, g_ref, o_ref):
    x = x_ref[...]
    ms = jnp.mean(x * x, axis=-1, keepdims=True)
    o_ref[...] = (x * lax.rsqrt(ms + EPS) * g_ref[...]).astype(o_ref.dtype)


def _rmsnorm(x, g, out_dtype, tr=512):
    n, d = x.shape
    tr = min(tr, n)
    return pl.pallas_call(
        _rmsnorm_kernel,
        grid=(n // tr,),
        in_specs=[pl.BlockSpec((tr, d), lambda i: (i, 0)), pl.BlockSpec((1, d), lambda i: (0, 0))],
        out_specs=pl.BlockSpec((tr, d), lambda i: (i, 0)),
        out_shape=jax.ShapeDtypeStruct((n, d), out_dtype),
        compiler_params=_params("parallel"),
        name="rmsnorm",
    )(x, g.reshape(1, d))


def _mm_kernel(*refs, nk, n_extra, epilogue, group_width):
    a_ref, w_ref = refs[0], refs[1]
    extra = refs[2:2 + n_extra]
    o_ref = refs[2 + n_extra]

    def emit(acc):
        res = epilogue(acc, *[e[...] for e in extra]).astype(o_ref.dtype)
        if group_width:
            for g in range(res.shape[1] // group_width):
                o_ref[g] = res[:, g * group_width:(g + 1) * group_width]
        else:
            o_ref[...] = res

    if nk == 1:
        emit(jnp.dot(a_ref[...], w_ref[...].astype(BF16), preferred_element_type=F32))
        return
    acc_ref = refs[3 + n_extra]
    k = pl.program_id(2)

    @pl.when(k == 0)
    def _():
        acc_ref[...] = jnp.zeros_like(acc_ref)

    acc_ref[...] += jnp.dot(a_ref[...], w_ref[...].astype(BF16), preferred_element_type=F32)

    @pl.when(k == nk - 1)
    def _():
        emit(acc_ref[...])


def _wcols(w, wcols, tn):
    col0, n = wcols if wcols else (0, w.shape[1])
    tn = min(tn, n)
    assert col0 % tn == 0 and n % tn == 0
    return col0 // tn, n, tn


def _matmul(a, w, out_dtype, *, epilogue=None, extras=(), tm=1024, tn=1024, tk=4096, group_width=0, wcols=None,
            name="matmul"):
    m, kd = a.shape
    jb0, n, tn = _wcols(w, wcols, tn)
    tm, tk = min(tm, m), min(tk, kd)
    nk = kd // tk
    if epilogue is None:
        epilogue = lambda acc: acc
    in_specs = [pl.BlockSpec((tm, tk), lambda i, j, k: (i, k)), pl.BlockSpec((tk, tn), lambda i, j, k: (k, j + jb0))]
    args = [a, w]
    for ex in extras:
        if ex[0] == "row":
            in_specs.append(pl.BlockSpec((1, tn), lambda i, j, k: (0, j)))
        else:
            off = ex[2] // tn
            in_specs.append(pl.BlockSpec((tm, tn), lambda i, j, k, off=off: (i, j + off)))
        args.append(ex[1])
    if group_width:
        gpt = tn // group_width
        out_specs = pl.BlockSpec((gpt, tm, group_width), lambda i, j, k: (j, i, 0))
        out_shape = jax.ShapeDtypeStruct((n // group_width, m, group_width), out_dtype)
    else:
        out_specs = pl.BlockSpec((tm, tn), lambda i, j, k: (i, j))
        out_shape = jax.ShapeDtypeStruct((m, n), out_dtype)
    return pl.pallas_call(
        functools.partial(_mm_kernel, nk=nk, n_extra=len(extras), epilogue=epilogue, group_width=group_width),
        grid=(m // tm, n // tn, nk),
        in_specs=in_specs,
        out_specs=out_specs,
        out_shape=out_shape,
        scratch_shapes=[pltpu.VMEM((tm, tn), F32)] if nk > 1 else [],
        compiler_params=_params("parallel", "parallel", "arbitrary"),
        name=name,
    )(*args)


def _mm_t_kernel(a_ref, w_ref, o_ref, *, DH, ones_rows):
    res = jnp.dot(a_ref[...], w_ref[...].astype(BF16), preferred_element_type=F32).T.astype(o_ref.dtype)
    if not ones_rows:
        o_ref[...] = res
        return
    VA = DH + ones_rows
    for h in range(res.shape[0] // DH):
        o_ref[h * VA:h * VA + DH, :] = res[h * DH:(h + 1) * DH, :]
        o_ref[h * VA + DH:(h + 1) * VA, :] = jnp.ones((ones_rows, res.shape[1]), o_ref.dtype)


def _matmul_t(a, w, out_dtype, *, tm=1024, tn=1024, DH=0, ones_rows=0, wcols=None, name="matmul_t"):
    m, kd = a.shape
    jb0, n, tn = _wcols(w, wcols, tn)
    tm = min(tm, m)
    rows = tn // DH * (DH + ones_rows) if ones_rows else tn
    return pl.pallas_call(
        functools.partial(_mm_t_kernel, DH=DH, ones_rows=ones_rows),
        grid=(m // tm, n // tn),
        in_specs=[pl.BlockSpec((tm, kd), lambda i, j: (i, 0)), pl.BlockSpec((kd, tn), lambda i, j: (0, j + jb0))],
        out_specs=pl.BlockSpec((rows, tm), lambda i, j: (j, i)),
        out_shape=jax.ShapeDtypeStruct((n // tn * rows, m), out_dtype),
        compiler_params=_params("parallel", "parallel"),
        name=name,
    )(a, w)


def _mm_kouter_kernel(a_ref, w_ref, r_ref, o_ref, acc_ref, *, nk, tn):
    k = pl.program_id(1)
    cs = pl.ds(pl.multiple_of(pl.program_id(2) * tn, tn), tn)
    part = jnp.dot(a_ref[...], w_ref[...].astype(BF16), preferred_element_type=F32)

    @pl.when(k == 0)
    def _():
        acc_ref[:, cs] = part

    @pl.when(jnp.logical_and(k > 0, k < nk - 1))
    def _():
        acc_ref[:, cs] += part

    @pl.when(k == nk - 1)
    def _():
        o_ref[...] = (acc_ref[:, cs] + part + r_ref[...]).astype(o_ref.dtype)


def _matmul_residual_kouter(a, w, res, out_dtype, *, tm=1024, tn=512, tk=4096, name="matmul_kouter"):
    m, kd = a.shape
    n = w.shape[1]
    tm, tn, tk = min(tm, m), min(tn, n), min(tk, kd)
    nk = kd // tk
    if nk < 2:
        return _matmul(a, w, out_dtype, epilogue=lambda acc, r: acc + r, extras=[("tile", res, 0)], tm=tm, tn=tn,
                       name=name)
    last_j = lambda i, k, j: (i, jnp.where(k == nk - 1, j, 0))
    return pl.pallas_call(
        functools.partial(_mm_kouter_kernel, nk=nk, tn=tn),
        grid=(m // tm, nk, n // tn),
        in_specs=[
            pl.BlockSpec((tm, tk), lambda i, k, j: (i, k)),
            pl.BlockSpec((tk, tn), lambda i, k, j: (k, j)),
            pl.BlockSpec((tm, tn), last_j),
        ],
        out_specs=pl.BlockSpec((tm, tn), last_j),
        out_shape=jax.ShapeDtypeStruct((m, n), out_dtype),
        scratch_shapes=[pltpu.VMEM((tm, n), F32)],
        compiler_params=_params("parallel", "arbitrary", "arbitrary"),
        name=name,
    )(a, w, res)


def _mix_kernel(a1_ref, w1_ref, a2_ref, w2_ref, g1_ref, g2_ref, o_ref):
    y1 = jnp.dot(a1_ref[...], w1_ref[...].astype(BF16), preferred_element_type=F32)
    y2 = jnp.dot(a2_ref[...], w2_ref[...].astype(BF16), preferred_element_type=F32)
    o_ref[...] = (g1_ref[...].astype(F32) * y1 + g2_ref[...].astype(F32) * y2).astype(o_ref.dtype)


def _gated_mix(a1, w1, a2, w2, gates, out_dtype, tm=1024, tn=1024):
    m, k1 = a1.shape
    k2 = a2.shape[1]
    n = w1.shape[1]
    tm, tn = min(tm, m), min(tn, n)
    nb = n // tn
    return pl.pallas_call(
        _mix_kernel,
        grid=(m // tm, nb),
        in_specs=[
            pl.BlockSpec((tm, k1), lambda i, j: (i, 0)),
            pl.BlockSpec((k1, tn), lambda i, j: (0, j)),
            pl.BlockSpec((tm, k2), lambda i, j: (i, 0)),
            pl.BlockSpec((k2, tn), lambda i, j: (0, j)),
            pl.BlockSpec((tm, tn), lambda i, j: (i, j)),
            pl.BlockSpec((tm, tn), lambda i, j: (i, j + nb)),
        ],
        out_specs=pl.BlockSpec((tm, tn), lambda i, j: (i, j)),
        out_shape=jax.ShapeDtypeStruct((m, n), out_dtype),
        compiler_params=_params("parallel", "parallel"),
        name="gated_mix",
    )(a1, w1, a2, w2, gates, gates)


def _conv_kernel(a_ref, g_ref, ah_ref, gh_ref, w_ref, cb_ref, lng_ref, lnb_ref, o_ref, hbuf, cbuf, sh,
                 *, T, C, KW, HALO, RC):
    i = pl.program_id(1)
    hh = ah_ref[...] * _sigmoid(gh_ref[...])
    hbuf[0:HALO, :] = jnp.where(i > 0, hh, 0.0)
    hbuf[HALO:HALO + T, :] = a_ref[...] * _sigmoid(g_ref[...])
    base = HALO - (KW - 1)
    NR = T + HALO - SUBLANES

    def chunk(c, carry):
        cs = pl.ds(pl.multiple_of(c * LANES, LANES), LANES)
        for b in range(1, SUBLANES):
            sh[b - 1, 0:NR, :] = hbuf[b:b + NR, cs]
        for r in range(T // RC):
            acc = jnp.zeros((RC, LANES), F32)
            for j in range(KW):
                a8, b = divmod(base + j, SUBLANES)
                r0 = r * RC + a8 * SUBLANES
                src = hbuf[r0:r0 + RC, cs] if b == 0 else sh[b - 1, r0:r0 + RC, :]
                acc = acc + w_ref[j:j + 1, cs] * src
            cbuf[r * RC:(r + 1) * RC, cs] = acc + cb_ref[:, cs]
        return carry

    lax.fori_loop(0, C // LANES, chunk, 0)

    RN = min(64, T)

    def norm(r, carry):
        rs = pl.ds(pl.multiple_of(r * RN, RN), RN)
        x = cbuf[rs, :]
        mu = jnp.mean(x, axis=-1, keepdims=True)
        xc = x - mu
        var = jnp.mean(xc * xc, axis=-1, keepdims=True)
        y = xc * lax.rsqrt(var + EPS) * lng_ref[...] + lnb_ref[...]
        o_ref[rs, :] = (y * _sigmoid(y)).astype(o_ref.dtype)
        return carry

    lax.fori_loop(0, T // RN, norm, 0)


def _conv_branch(p_conv, conv_w, conv_bias, ln_g, ln_b, batch, seq, T=256):
    n, c2 = p_conv.shape
    C = c2 // 2
    KW = conv_w.shape[0]
    HALO = 32
    assert KW - 1 <= HALO
    T = min(T, seq)
    RC = min(64, T)
    nt = seq // T
    hb = T // HALO
    row = lambda b, i: b * nt + i
    halo_row = lambda b, i: jnp.maximum((b * nt + i) * hb - 1, 0)
    vec = lambda v: v.reshape(1, C)
    return pl.pallas_call(
        functools.partial(_conv_kernel, T=T, C=C, KW=KW, HALO=HALO, RC=RC),
        grid=(batch, nt),
        in_specs=[
            pl.BlockSpec((T, C), lambda b, i: (row(b, i), 0)),
            pl.BlockSpec((T, C), lambda b, i: (row(b, i), 1)),
            pl.BlockSpec((HALO, C), lambda b, i: (halo_row(b, i), 0)),
            pl.BlockSpec((HALO, C), lambda b, i: (halo_row(b, i), 1)),
            pl.BlockSpec((KW, C), lambda b, i: (0, 0)),
            pl.BlockSpec((1, C), lambda b, i: (0, 0)),
            pl.BlockSpec((1, C), lambda b, i: (0, 0)),
            pl.BlockSpec((1, C), lambda b, i: (0, 0)),
        ],
        out_specs=pl.BlockSpec((T, C), lambda b, i: (row(b, i), 0)),
        out_shape=jax.ShapeDtypeStruct((n, C), BF16),
        scratch_shapes=[pltpu.VMEM((HALO + T, C), F32), pltpu.VMEM((T, C), F32),
                        pltpu.VMEM((SUBLANES - 1, HALO + T, LANES), F32)],
        compiler_params=_params("parallel", "parallel"),
        name="conv_branch",
    )(p_conv, p_conv, p_conv, p_conv, conv_w, vec(conv_bias), vec(ln_g), vec(ln_b))


def _index_mask_kernel(q_ref, kt_ref, w_ref, o_ref, sc, hi16, lo16, wb, rhs,
                       *, TQ, TS, S, G, HG, DI, GW, topk, idx_bits):
    i = pl.program_id(1)
    NH = G * HG
    CR = GW * TQ
    ntrips = (i + GW) // GW
    NACC = 4
    int_min = jnp.int32(-2 ** 31)

    for h in range(NH):
        wb[h] = jnp.broadcast_to(w_ref[:, h:h + 1], (TQ, TS))
    rhs[...] = jnp.zeros_like(rhs)
    q = q_ref[...].reshape(G * TQ, HG * DI)
    t_pos_c = i * TQ + lax.broadcasted_iota(jnp.int32, (TS, TQ), 1)
    row_c = lax.broadcasted_iota(jnp.int32, (TS, TQ), 0)

    def score_chunk(c, slot):
        cs = pl.ds(pl.multiple_of(c * TS, TS), TS)
        kt = kt_ref[0, :, cs]
        for h in range(HG):
            rhs[slot, h * DI:(h + 1) * DI, h * TS:(h + 1) * TS] = kt
        x = jnp.dot(q, rhs[slot], preferred_element_type=F32)
        acc = jnp.zeros((TQ, TS), F32)
        for g in range(G):
            for h in range(HG):
                acc = acc + wb[g * HG + h] * jnp.maximum(x[g * TQ:(g + 1) * TQ, h * TS:(h + 1) * TS], 0.0)
        acc = acc.T + 0.0
        bits = pltpu.bitcast(acc, jnp.int32)
        key = jnp.where(bits < 0, bits ^ jnp.int32(0x7FFFFFFF), bits)
        key = jnp.where(c * TS + row_c <= t_pos_c, key, int_min)
        sc[cs, :] = key
        hi16[cs, :] = jnp.right_shift(key, 16).astype(jnp.int16)

    CPT = CR // TS

    def score_group(cg, carry):
        for u in range(CPT):
            score_chunk(cg * CPT + u, u)
        return carry

    lax.fori_loop(0, ntrips, score_group, 0)

    t_pos = i * TQ + lax.broadcasted_iota(jnp.int32, (CR, TQ), 1)
    row = lax.broadcasted_iota(jnp.int32, (CR, TQ), 0)
    row3 = row.reshape(CR // SUBLANES, SUBLANES, TQ)

    def count(pred):
        def body(c, cnt):
            k3 = sc[pl.ds(pl.multiple_of(c * CR, CR), CR), :].reshape(CR // SUBLANES, SUBLANES, TQ)
            hit = pred(k3, c * CR + row3).astype(jnp.int32)
            return cnt + jnp.sum(hit.reshape(CR // (SUBLANES * NACC), NACC, SUBLANES, TQ), axis=0)

        cnt = lax.fori_loop(0, ntrips, body, jnp.zeros((NACC, SUBLANES, TQ), jnp.int32))
        return jnp.sum(cnt.reshape(NACC * SUBLANES, TQ), axis=0, keepdims=True)

    bc = lambda v: jnp.broadcast_to(v, (SUBLANES, TQ))[None]
    bc16 = lambda v: jnp.broadcast_to(v, (BF16_ROWS, TQ)).astype(jnp.int16)[None]
    G16 = CR // (BF16_ROWS * NACC)
    assert (S // CR) * G16 < 2 ** 15
    min16 = -2 ** 15

    def count16(ref, cand):
        cand_b = bc16(cand)

        def body(c, cnt):
            k3 = ref[pl.ds(pl.multiple_of(c * CR, CR), CR), :].reshape(CR // BF16_ROWS, BF16_ROWS, TQ)
            hit = jnp.where(k3 >= cand_b, jnp.int16(1), jnp.int16(0)).reshape(G16, NACC, BF16_ROWS, TQ)
            for g in range(G16):
                cnt = cnt + hit[g]
            return cnt

        cnt = lax.fori_loop(0, ntrips, body, jnp.zeros((NACC, BF16_ROWS, TQ), jnp.int16))
        return jnp.sum(cnt.astype(jnp.int32).reshape(NACC * BF16_ROWS, TQ), axis=0, keepdims=True)

    def search16(ref, need):
        def bit_step(b, cur):
            cand = cur + jnp.left_shift(jnp.int32(1), 15 - b)
            return jnp.where(count16(ref, cand) >= need, cand, cur)

        return lax.fori_loop(0, 16, bit_step, jnp.full((1, TQ), min16, jnp.int32))

    thr_hi = search16(hi16, topk)
    n_above = jnp.where(thr_hi == 2 ** 15 - 1, 0, count16(hi16, jnp.minimum(thr_hi + 1, 2 ** 15 - 1)))

    def fill_lo(c, carry):
        rs = pl.ds(pl.multiple_of(c * CR, CR), CR)
        k = sc[rs, :]
        lo = jnp.bitwise_and(k, 0xFFFF) + min16
        lo16[rs, :] = jnp.where(jnp.right_shift(k, 16) == thr_hi, lo, min16).astype(jnp.int16)
        return carry

    lax.fori_loop(0, ntrips, fill_lo, 0)
    thr_lo = search16(lo16, topk - n_above)
    thr = thr_hi * 2 ** 16 + (thr_lo - min16)
    thr_b = bc(thr)
    n_ge = count(lambda k, s: k >= thr_b)
    has_ties = jnp.max(jnp.where(thr > int_min, n_ge, 0)) > topk

    def write(sel):
        def body(c, carry):
            rs = pl.ds(pl.multiple_of(c * CR, CR), CR)
            s_pos = c * CR + row
            keep = sel(sc[rs, :], s_pos) & (s_pos <= t_pos)
            o_ref[0, rs, :] = jnp.where(keep, 0.0, MASKED).astype(o_ref.dtype)
            return carry

        lax.fori_loop(0, ntrips, body, 0)

    @pl.when(jnp.logical_not(has_ties))
    def _():
        write(lambda k, s_pos: k >= thr)

    @pl.when(has_ties)
    def _():
        need = bc(topk - count(lambda k, s: k > thr_b))

        def idx_step(b, lim):
            cand = lim + jnp.left_shift(jnp.int32(1), idx_bits - 1 - b)
            cand_b = bc(cand)
            n_before = count(lambda k, s: (k == thr_b) & (s < cand_b))
            return jnp.where(n_before < need[0, 0:1], cand, lim)

        lim = lax.fori_loop(0, idx_bits, idx_step, jnp.zeros((1, TQ), jnp.int32))
        write(lambda k, s_pos: (k > thr) | ((k == thr) & (s_pos <= lim)))

    def clear(c, carry):
        o_ref[0, pl.ds(pl.multiple_of(c * CR, CR), CR), :] = jnp.full((CR, TQ), MASKED, o_ref.dtype)
        return carry

    lax.fori_loop(ntrips, S // CR, clear, 0)


def _index_mask(qi_g, ki_t, wi, batch, seq, topk, TQ=256, TS=128):
    G, n, gw = qi_g.shape
    DI = ki_t.shape[1]
    HG = gw // DI
    NH = G * HG
    TQ = min(TQ, seq)
    TS = min(TS, TQ)
    nq = seq // TQ
    GW = 2 if nq % 2 == 0 else 1
    idx_bits = max(1, int(seq - 1).bit_length())
    return pl.pallas_call(
        functools.partial(_index_mask_kernel, TQ=TQ, TS=TS, S=seq, G=G, HG=HG, DI=DI, GW=GW, topk=topk,
                          idx_bits=idx_bits),
        grid=(batch, nq),
        in_specs=[
            pl.BlockSpec((G, TQ, gw), lambda b, i: (0, b * nq + i, 0)),
            pl.BlockSpec((1, DI, seq), lambda b, i: (b, 0, 0)),
            pl.BlockSpec((TQ, NH), lambda b, i: (b * nq + i, 0)),
        ],
        out_specs=pl.BlockSpec((1, seq, TQ), lambda b, i: (b, 0, i)),
        out_shape=jax.ShapeDtypeStruct((batch, seq, seq), BF16),
        scratch_shapes=[
            pltpu.VMEM((seq, TQ), jnp.int32),
            pltpu.VMEM((seq, TQ), jnp.int16),
            pltpu.VMEM((seq, TQ), jnp.int16),
            pltpu.VMEM((NH, TQ, TS), F32),
            pltpu.VMEM((GW * TQ // TS, HG * DI, HG * TS), BF16),
        ],
        compiler_params=_params("parallel", "parallel"),
        name="index_mask",
    )(qi_g, ki_t, wi)


def _attn_kernel(ii_ref, jj_ref, qt_ref, k_ref, vt_ref, mask_ref, bias_ref, bfar_ref, o_ref,
                 m_scr, r_scr, a_scr, z_scr, acc_scr, *, H, DH, VA, NEAR, KT, T, c2):
    pair = pl.program_id(1)
    i = ii_ref[pair]
    jg = jj_ref[pair]

    @pl.when(jg == 0)
    def _():
        m_scr[...] = jnp.full_like(m_scr, MASKED)
        acc_scr[...] = jnp.zeros_like(acc_scr)

    def key_tile(sub, carry):
        j = jg * KT + sub
        ks = pl.ds(pl.multiple_of(sub * T, T), T)

        def scores(near):
            mask = mask_ref[0, ks, :].astype(F32)
            for h in range(H):
                hs = slice(h * DH, (h + 1) * DH)
                z = jnp.dot(k_ref[ks, hs], qt_ref[hs, :], preferred_element_type=F32) * c2 + mask
                m_old = m_scr[h]
                if near is not None:
                    z = z + bias_ref[h, near]
                    r = jnp.maximum(m_old, jnp.max(z, axis=0, keepdims=True)).astype(BF16)
                    m_new = r.astype(F32)
                else:
                    shift = bfar_ref[h]
                    r = (jnp.maximum(m_old, jnp.max(z, axis=0, keepdims=True) + shift) - shift).astype(BF16)
                    m_new = r.astype(F32) + shift
                z_scr[h] = z.astype(BF16)
                a_scr[h] = jnp.exp2(m_old - m_new)
                r_scr[h] = jnp.broadcast_to(r, (BF16_ROWS, T))
                m_scr[h] = m_new

        for d in range(NEAR):
            pl.when(i - j == d)(functools.partial(scores, d))
        pl.when(i - j >= NEAR)(functools.partial(scores, None))

        for h in range(H):
            z3 = z_scr[h].reshape(T // BF16_ROWS, BF16_ROWS, T)
            p = jnp.exp2(z3 - r_scr[h][None]).reshape(T, T)
            pv = jnp.dot(vt_ref[h * VA:(h + 1) * VA, ks], p, preferred_element_type=F32)
            acc_scr[h] = a_scr[h] * acc_scr[h] + pv
        return carry

    lax.fori_loop(0, jnp.minimum(KT, i - jg * KT + 1), key_tile, 0)

    @pl.when(jg * KT + KT > i)
    def _():
        for h in range(H):
            a = acc_scr[h]
            o_ref[:, h * DH:(h + 1) * DH] = (a[:DH] / a[DH:DH + 1]).T.astype(o_ref.dtype)


def _attention(q_t, k, vt_aug, mask_t, bias_t, bias_far, batch, seq, n_heads, T):
    width, n = q_t.shape
    DH = width // n_heads
    VA = vt_aug.shape[0] // n_heads
    nt = seq // T
    NEAR = bias_t.shape[1]
    c2 = (DH ** -0.5) * math.log2(math.e)
    KT = next(c for c in (4, 2, 1) if nt % c == 0)
    ng = nt // KT
    pairs = [(i, jg) for i in range(nt) for jg in range(i // KT + 1)]
    ii = jnp.asarray([p[0] for p in pairs], jnp.int32)
    jj = jnp.asarray([p[1] for p in pairs], jnp.int32)
    stat = pltpu.VMEM((n_heads, 1, T), F32)
    grid_spec = pltpu.PrefetchScalarGridSpec(
        num_scalar_prefetch=2,
        grid=(batch, len(pairs)),
        in_specs=[
            pl.BlockSpec((width, T), lambda b, p, ii, jj: (0, b * nt + ii[p])),
            pl.BlockSpec((KT * T, width), lambda b, p, ii, jj: (b * ng + jj[p], 0)),
            pl.BlockSpec((n_heads * VA, KT * T), lambda b, p, ii, jj: (0, b * ng + jj[p])),
            pl.BlockSpec((1, KT * T, T), lambda b, p, ii, jj: (b, jj[p], ii[p])),
            pl.BlockSpec(bias_t.shape, lambda b, p, ii, jj: (0, 0, 0, 0)),
            pl.BlockSpec(bias_far.shape, lambda b, p, ii, jj: (0, 0, 0)),
        ],
        out_specs=pl.BlockSpec((T, width), lambda b, p, ii, jj: (b * nt + ii[p], 0)),
        scratch_shapes=[stat, pltpu.VMEM((n_heads, BF16_ROWS, T), BF16), stat,
                        pltpu.VMEM((n_heads, T, T), BF16), pltpu.VMEM((n_heads, VA, T), F32)],
    )
    return pl.pallas_call(
        functools.partial(_attn_kernel, H=n_heads, DH=DH, VA=VA, NEAR=NEAR, KT=KT, T=T, c2=c2),
        grid_spec=grid_spec,
        out_shape=jax.ShapeDtypeStruct((n, width), BF16),
        compiler_params=_params("parallel", "arbitrary"),
        name="masked_attention",
    )(ii, jj, q_t, k, vt_aug, mask_t, bias_t, bias_far)


def _t5_bucket(n):
    max_exact = N_BUCKETS // 2
    nf = jnp.maximum(n, 1).astype(F32)
    large = max_exact + (jnp.log(nf / max_exact) / math.log(MAX_DISTANCE / max_exact)
                         * (N_BUCKETS - max_exact)).astype(jnp.int32)
    large = jnp.minimum(large, N_BUCKETS - 1)
    return jnp.where(n < max_exact, n, large)


def _bias_tiles(rel_bias, T, near):
    s = jnp.arange(T, dtype=jnp.int32)[:, None]
    t = jnp.arange(T, dtype=jnp.int32)[None, :]
    rb = rel_bias.astype(F32) * math.log2(math.e)
    tiles = []
    for d in range(near):
        bucket = _t5_bucket(jnp.maximum(d * T + t - s, 0))
        onehot = bucket[None, :, :] == jnp.arange(N_BUCKETS, dtype=jnp.int32)[:, None, None]
        tiles.append(jnp.sum(jnp.where(onehot[:, None], rb[:, :, None, None], 0.0), axis=0))
    far = jnp.broadcast_to(rb[N_BUCKETS - 1][:, None, None], (rb.shape[1], 1, T))
    return jnp.stack(tiles, axis=1), far


def kernel(x, norm1_g, w_in, b_gate, conv_w, conv_bias, conv_ln_g, conv_ln_b, w_conv_out, w_attn_out,
           rel_bias, w_o, norm2_g, w_ff1, w_ff2, normf_g):
    B, S, D = x.shape
    N = B * S
    depth = w_in.shape[0]
    C = conv_w.shape[-1]
    A = w_attn_out.shape[1]
    H = rel_bias.shape[1]
    DH = A // H
    NH, DI = IDX_HEADS, IDX_DIM
    HG = MXU_DIM // DI
    topk = min(TOPK_MAX, S // 4)
    off_q = 2 * C
    off_iq = off_q + 3 * A
    off_ik = off_iq + NH * DI
    off_iw = off_ik + DI
    off_gate = off_iw + NH
    T_ATT = min(256, S)
    near = -(-(MAX_DISTANCE + T_ATT - 1) // T_ATT)
    bias_t, bias_far = _bias_tiles(rel_bias, T_ATT, near)
    sigmoid_bias = lambda acc, b: _sigmoid(acc + b)
    relu2 = lambda acc: jnp.square(jnp.maximum(acc, 0.0))
    add = lambda acc, r: acc + r

    h = x.reshape(N, D)
    for l in range(depth):
        wl = w_in[l]
        u = _rmsnorm(h, norm1_g[l], BF16)
        wb16 = wl.astype(BF16)
        p_conv = _matmul(u, wb16, F32, wcols=(0, off_q), name="proj_conv")
        q_t = _matmul_t(u, wb16, BF16, wcols=(off_q, A), name="proj_q")
        k = _matmul(u, wb16, BF16, wcols=(off_q + A, A), name="proj_k")
        vt_aug = _matmul_t(u, wb16, BF16, wcols=(off_q + 2 * A, A), DH=DH, ones_rows=BF16_ROWS, name="proj_v")
        qi_g = _matmul(u, wb16, BF16, wcols=(off_iq, NH * DI), group_width=HG * DI, name="proj_qi")
        kw = _matmul(u, wb16, F32, wcols=(off_ik, LANES), name="proj_ki_wi")
        gates = _matmul(u, wb16[:, off_gate:], BF16, epilogue=sigmoid_bias,
                        extras=[("row", b_gate[l].reshape(1, -1))], name="proj_gates")

        cf = _conv_branch(p_conv, conv_w[l], conv_bias[l], conv_ln_g[l], conv_ln_b[l], B, S)

        ki_t = jnp.transpose(kw[:, :DI].astype(BF16).reshape(B, S, DI), (0, 2, 1))
        wi = kw[:, DI:DI + NH] * (NH ** -0.5) * (DI ** -0.5)
        mask_t = _index_mask(qi_g, ki_t, wi, B, S, topk)
        o = _attention(q_t, k, vt_aug, mask_t, bias_t, bias_far, B, S, H, T_ATT)

        mixed = _gated_mix(cf, w_conv_out[l].astype(BF16), o, w_attn_out[l].astype(BF16), gates, BF16)
        h = _matmul(mixed, w_o[l].astype(BF16), F32, epilogue=add, extras=[("tile", h, 0)], name="w_o")
        u2 = _rmsnorm(h, norm2_g[l], BF16)
        act = _matmul(u2, w_ff1[l], BF16, epilogue=relu2, name="ff1")
        h = _matmul_residual_kouter(act, w_ff2[l].astype(BF16), h, F32, tm=512, tk=FF2_TK, name="ff2")
    return _rmsnorm(h, normf_g, F32).reshape(B, S, D)
```

```python
import functools
import math

import jax
import jax.numpy as jnp
from jax import lax
from jax.experimental import pallas as pl
from jax.experimental.pallas import tpu as pltpu

IDX_HEADS = 32
IDX_DIM = 64
TOPK_MAX = 256
N_BUCKETS = 32
MAX_DISTANCE = 128
EPS = 1e-6

V7X_VMEM_BYTES = 64 * 1024 * 1024
VMEM_LIMIT_BYTES = V7X_VMEM_BYTES - 2 * 1024 * 1024
LANES = 128
SUBLANES = 8
BF16_ROWS = 16
MXU_DIM = 256
MASKED = -(2.0 ** 100)

F32 = jnp.float32
BF16 = jnp.bfloat16


def _params(*sem):
    return pltpu.CompilerParams(dimension_semantics=sem, vmem_limit_bytes=VMEM_LIMIT_BYTES)


def _sigmoid(x):
    return 0.5 * jnp.tanh(0.5 * x) + 0.5


def _rmsnorm_kernel(x_ref, g_ref, o_ref):
    x = x_ref[...]
    ms = jnp.mean(x * x, axis=-1, keepdims=True)
    o_ref[...] = (x * lax.rsqrt(ms + EPS) * g_ref[...]).astype(o_ref.dtype)


def _rmsnorm(x, g, out_dtype, tr=512):
    n, d = x.shape
    tr = min(tr, n)
    return pl.pallas_call(
        _rmsnorm_kernel,
        grid=(n // tr,),
        in_specs=[pl.BlockSpec((tr, d), lambda i: (i, 0)), pl.BlockSpec((1, d), lambda i: (0, 0))],
        out_specs=pl.BlockSpec((tr, d), lambda i: (i, 0)),
        out_shape=jax.ShapeDtypeStruct((n, d), out_dtype),
        compiler_params=_params("parallel"),
        name="rmsnorm",
    )(x, g.reshape(1, d))


def _mm_kernel(*refs, nk, n_extra, epilogue, group_width, side):
    a_ref, w_ref = refs[0], refs[1]
    first = 3 if side else 2
    extra = refs[first:first + n_extra]
    o_ref = refs[first + n_extra]

    def emit(acc):
        res = epilogue(acc, *[e[...] for e in extra]).astype(o_ref.dtype)
        if group_width:
            for g in range(res.shape[1] // group_width):
                o_ref[g] = res[:, g * group_width:(g + 1) * group_width]
        else:
            o_ref[...] = res

    if nk == 1:
        emit(jnp.dot(a_ref[...], w_ref[...].astype(BF16), preferred_element_type=F32))
        if side:
            ws_ref, os_ref = refs[2], refs[first + n_extra + 1]

            @pl.when(pl.program_id(1) == 0)
            def _():
                os_ref[...] = jnp.dot(a_ref[...], ws_ref[...].astype(BF16),
                                      preferred_element_type=F32).astype(os_ref.dtype)
        return
    acc_ref = refs[3 + n_extra]
    k = pl.program_id(2)

    @pl.when(k == 0)
    def _():
        acc_ref[...] = jnp.zeros_like(acc_ref)

    acc_ref[...] += jnp.dot(a_ref[...], w_ref[...].astype(BF16), preferred_element_type=F32)

    @pl.when(k == nk - 1)
    def _():
        emit(acc_ref[...])


def _wcols(w, wcols, tn):
    col0, n = wcols if wcols else (0, w.shape[1])
    tn = min(tn, n)
    assert col0 % tn == 0 and n % tn == 0
    return col0 // tn, n, tn


def _matmul(a, w, out_dtype, *, epilogue=None, extras=(), tm=1024, tn=1024, tk=4096, group_width=0, wcols=None,
            side=None, name="matmul"):
    m, kd = a.shape
    jb0, n, tn = _wcols(w, wcols, tn)
    tm, tk = min(tm, m), min(tk, kd)
    nk = kd // tk
    if epilogue is None:
        epilogue = lambda acc: acc
    in_specs = [pl.BlockSpec((tm, tk), lambda i, j, k: (i, k)), pl.BlockSpec((tk, tn), lambda i, j, k: (k, j + jb0))]
    args = [a, w]
    if side:
        s0, n2, side_dtype = side
        assert nk == 1 and s0 % n2 == 0
        in_specs.append(pl.BlockSpec((tk, n2), lambda i, j, k: (0, s0 // n2)))
        args.append(w)
    for ex in extras:
        if ex[0] == "row":
            in_specs.append(pl.BlockSpec((1, tn), lambda i, j, k: (0, j)))
        else:
            off = ex[2] // tn
            in_specs.append(pl.BlockSpec((tm, tn), lambda i, j, k, off=off: (i, j + off)))
        args.append(ex[1])
    if group_width:
        gpt = tn // group_width
        out_specs = pl.BlockSpec((gpt, tm, group_width), lambda i, j, k: (j, i, 0))
        out_shape = jax.ShapeDtypeStruct((n // group_width, m, group_width), out_dtype)
    else:
        out_specs = pl.BlockSpec((tm, tn), lambda i, j, k: (i, j))
        out_shape = jax.ShapeDtypeStruct((m, n), out_dtype)
    if side:
        out_specs = [out_specs, pl.BlockSpec((tm, n2), lambda i, j, k: (i, 0))]
        out_shape = [out_shape, jax.ShapeDtypeStruct((m, n2), side_dtype)]
    return pl.pallas_call(
        functools.partial(_mm_kernel, nk=nk, n_extra=len(extras), epilogue=epilogue, group_width=group_width,
                          side=bool(side)),
        grid=(m // tm, n // tn, nk),
        in_specs=in_specs,
        out_specs=out_specs,
        out_shape=out_shape,
        scratch_shapes=[pltpu.VMEM((tm, tn), F32)] if nk > 1 else [],
        compiler_params=_params("parallel", "arbitrary" if side else "parallel", "arbitrary"),
        name=name,
    )(*args)


def _mm_t_kernel(a_ref, w_ref, o_ref, *, DH, ones_rows):
    res = jnp.dot(a_ref[...], w_ref[...].astype(BF16), preferred_element_type=F32).T.astype(o_ref.dtype)
    if not ones_rows:
        o_ref[...] = res
        return
    VA = DH + ones_rows
    for h in range(res.shape[0] // DH):
        o_ref[h * VA:h * VA + DH, :] = res[h * DH:(h + 1) * DH, :]
        o_ref[h * VA + DH:(h + 1) * VA, :] = jnp.ones((ones_rows, res.shape[1]), o_ref.dtype)


def _matmul_t(a, w, out_dtype, *, tm=1024, tn=1024, DH=0, ones_rows=0, wcols=None, name="matmul_t"):
    m, kd = a.shape
    jb0, n, tn = _wcols(w, wcols, tn)
    tm = min(tm, m)
    rows = tn // DH * (DH + ones_rows) if ones_rows else tn
    return pl.pallas_call(
        functools.partial(_mm_t_kernel, DH=DH, ones_rows=ones_rows),
        grid=(m // tm, n // tn),
        in_specs=[pl.BlockSpec((tm, kd), lambda i, j: (i, 0)), pl.BlockSpec((kd, tn), lambda i, j: (0, j + jb0))],
        out_specs=pl.BlockSpec((rows, tm), lambda i, j: (j, i)),
        out_shape=jax.ShapeDtypeStruct((n // tn * rows, m), out_dtype),
        compiler_params=_params("parallel", "parallel"),
        name=name,
    )(a, w)


def _mix_kernel(a1_ref, w1_ref, a2_ref, w2_ref, g1_ref, g2_ref, o_ref):
    y1 = jnp.dot(a1_ref[...], w1_ref[...].astype(BF16), preferred_element_type=F32)
    y2 = jnp.dot(a2_ref[...], w2_ref[...].astype(BF16), preferred_element_type=F32)
    o_ref[...] = (g1_ref[...].astype(F32) * y1 + g2_ref[...].astype(F32) * y2).astype(o_ref.dtype)


def _gated_mix(a1, w1, a2, w2, gates, out_dtype, tm=1024, tn=1024):
    m, k1 = a1.shape
    k2 = a2.shape[1]
    n = w1.shape[1]
    tm, tn = min(tm, m), min(tn, n)
    nb = n // tn
    return pl.pallas_call(
        _mix_kernel,
        grid=(m // tm, nb),
        in_specs=[
            pl.BlockSpec((tm, k1), lambda i, j: (i, 0)),
            pl.BlockSpec((k1, tn), lambda i, j: (0, j)),
            pl.BlockSpec((tm, k2), lambda i, j: (i, 0)),
            pl.BlockSpec((k2, tn), lambda i, j: (0, j)),
            pl.BlockSpec((tm, tn), lambda i, j: (i, j)),
            pl.BlockSpec((tm, tn), lambda i, j: (i, j + nb)),
        ],
        out_specs=pl.BlockSpec((tm, tn), lambda i, j: (i, j)),
        out_shape=jax.ShapeDtypeStruct((m, n), out_dtype),
        compiler_params=_params("parallel", "parallel"),
        name="gated_mix",
    )(a1, w1, a2, w2, gates, gates)


def _conv_kernel(a_ref, g_ref, ah_ref, gh_ref, w_ref, cb_ref, lng_ref, lnb_ref, o_ref, hbuf, cbuf, sh,
                 *, T, C, KW, HALO, RC):
    i = pl.program_id(1)
    hh = ah_ref[...] * _sigmoid(gh_ref[...])
    hbuf[0:HALO, :] = jnp.where(i > 0, hh, 0.0)
    hbuf[HALO:HALO + T, :] = a_ref[...] * _sigmoid(g_ref[...])
    base = HALO - (KW - 1)
    NR = T + HALO - SUBLANES

    def chunk(c, carry):
        cs = pl.ds(pl.multiple_of(c * LANES, LANES), LANES)
        for b in range(1, SUBLANES):
            sh[b - 1, 0:NR, :] = hbuf[b:b + NR, cs]
        for r in range(T // RC):
            acc = jnp.zeros((RC, LANES), F32)
            for j in range(KW):
                a8, b = divmod(base + j, SUBLANES)
                r0 = r * RC + a8 * SUBLANES
                src = hbuf[r0:r0 + RC, cs] if b == 0 else sh[b - 1, r0:r0 + RC, :]
                acc = acc + w_ref[j:j + 1, cs] * src
            cbuf[r * RC:(r + 1) * RC, cs] = acc + cb_ref[:, cs]
        return carry

    lax.fori_loop(0, C // LANES, chunk, 0)

    RN = min(64, T)

    def norm(r, carry):
        rs = pl.ds(pl.multiple_of(r * RN, RN), RN)
        x = cbuf[rs, :]
        mu = jnp.mean(x, axis=-1, keepdims=True)
        xc = x - mu
        var = jnp.mean(xc * xc, axis=-1, keepdims=True)
        y = xc * lax.rsqrt(var + EPS) * lng_ref[...] + lnb_ref[...]
        o_ref[rs, :] = (y * _sigmoid(y)).astype(o_ref.dtype)
        return carry

    lax.fori_loop(0, T // RN, norm, 0)


def _conv_branch(p_conv, conv_w, conv_bias, ln_g, ln_b, batch, seq, T=256):
    n, c2 = p_conv.shape
    C = c2 // 2
    KW = conv_w.shape[0]
    HALO = 32
    assert KW - 1 <= HALO
    T = min(T, seq)
    RC = min(64, T)
    nt = seq // T
    hb = T // HALO
    row = lambda b, i: b * nt + i
    halo_row = lambda b, i: jnp.maximum((b * nt + i) * hb - 1, 0)
    vec = lambda v: v.reshape(1, C)
    return pl.pallas_call(
        functools.partial(_conv_kernel, T=T, C=C, KW=KW, HALO=HALO, RC=RC),
        grid=(batch, nt),
        in_specs=[
            pl.BlockSpec((T, C), lambda b, i: (row(b, i), 0)),
            pl.BlockSpec((T, C), lambda b, i: (row(b, i), 1)),
            pl.BlockSpec((HALO, C), lambda b, i: (halo_row(b, i), 0)),
            pl.BlockSpec((HALO, C), lambda b, i: (halo_row(b, i), 1)),
            pl.BlockSpec((KW, C), lambda b, i: (0, 0)),
            pl.BlockSpec((1, C), lambda b, i: (0, 0)),
            pl.BlockSpec((1, C), lambda b, i: (0, 0)),
            pl.BlockSpec((1, C), lambda b, i: (0, 0)),
        ],
        out_specs=pl.BlockSpec((T, C), lambda b, i: (row(b, i), 0)),
        out_shape=jax.ShapeDtypeStruct((n, C), BF16),
        scratch_shapes=[pltpu.VMEM((HALO + T, C), F32), pltpu.VMEM((T, C), F32),
                        pltpu.VMEM((SUBLANES - 1, HALO + T, LANES), F32)],
        compiler_params=_params("parallel", "parallel"),
        name="conv_branch",
    )(p_conv, p_conv, p_conv, p_conv, conv_w, vec(conv_bias), vec(ln_g), vec(ln_b))


def _index_mask_kernel(q_ref, kt_ref, w_ref, o_ref, sc, hi16, lo16, wb, rhs,
                       *, TQ, TS, S, G, HG, DI, GW, topk, idx_bits):
    i = pl.program_id(1)
    NH = G * HG
    CR = GW * TQ
    ntrips = (i + GW) // GW
    NACC = 4
    int_min = jnp.int32(-2 ** 31)

    for h in range(NH):
        wb[h] = jnp.broadcast_to(w_ref[:, h:h + 1], (TQ, TS))
    rhs[...] = jnp.zeros_like(rhs)
    q = q_ref[...].reshape(G * TQ, HG * DI)
    t_pos_c = i * TQ + lax.broadcasted_iota(jnp.int32, (TS, TQ), 1)
    row_c = lax.broadcasted_iota(jnp.int32, (TS, TQ), 0)

    def score_chunk(c, slot):
        cs = pl.ds(pl.multiple_of(c * TS, TS), TS)
        kt = kt_ref[0, :, cs]
        for h in range(HG):
            rhs[slot, h * DI:(h + 1) * DI, h * TS:(h + 1) * TS] = kt
        x = jnp.dot(q, rhs[slot], preferred_element_type=F32)
        acc = jnp.zeros((TQ, TS), F32)
        for g in range(G):
            for h in range(HG):
                acc = acc + wb[g * HG + h] * jnp.maximum(x[g * TQ:(g + 1) * TQ, h * TS:(h + 1) * TS], 0.0)
        acc = acc.T + 0.0
        bits = pltpu.bitcast(acc, jnp.int32)
        key = jnp.where(bits < 0, bits ^ jnp.int32(0x7FFFFFFF), bits)
        key = jnp.where(c * TS + row_c <= t_pos_c, key, int_min)
        sc[cs, :] = key
        hi16[cs, :] = jnp.right_shift(key, 16).astype(jnp.int16)

    CPT = CR // TS

    def score_group(cg, carry):
        for u in range(CPT):
            score_chunk(cg * CPT + u, u)
        return carry

    lax.fori_loop(0, ntrips, score_group, 0)

    t_pos = i * TQ + lax.broadcasted_iota(jnp.int32, (CR, TQ), 1)
    row = lax.broadcasted_iota(jnp.int32, (CR, TQ), 0)
    row3 = row.reshape(CR // SUBLANES, SUBLANES, TQ)

    def count(pred):
        def body(c, cnt):
            k3 = sc[pl.ds(pl.multiple_of(c * CR, CR), CR), :].reshape(CR // SUBLANES, SUBLANES, TQ)
            hit = pred(k3, c * CR + row3).astype(jnp.int32)
            return cnt + jnp.sum(hit.reshape(CR // (SUBLANES * NACC), NACC, SUBLANES, TQ), axis=0)

        cnt = lax.fori_loop(0, ntrips, body, jnp.zeros((NACC, SUBLANES, TQ), jnp.int32))
        return jnp.sum(cnt.reshape(NACC * SUBLANES, TQ), axis=0, keepdims=True)

    bc = lambda v: jnp.broadcast_to(v, (SUBLANES, TQ))[None]
    bc16 = lambda v: jnp.broadcast_to(v, (BF16_ROWS, TQ)).astype(jnp.int16)[None]
    G16 = CR // (BF16_ROWS * NACC)
    assert (S // CR) * G16 < 2 ** 15
    min16 = -2 ** 15

    def count16(ref, cand):
        cand_b = bc16(cand)

        def body(c, cnt):
            k3 = ref[pl.ds(pl.multiple_of(c * CR, CR), CR), :].reshape(CR // BF16_ROWS, BF16_ROWS, TQ)
            hit = jnp.where(k3 >= cand_b, jnp.int16(1), jnp.int16(0)).reshape(G16, NACC, BF16_ROWS, TQ)
            for g in range(G16):
                cnt = cnt + hit[g]
            return cnt

        cnt = lax.fori_loop(0, ntrips, body, jnp.zeros((NACC, BF16_ROWS, TQ), jnp.int16))
        return jnp.sum(cnt.astype(jnp.int32).reshape(NACC * BF16_ROWS, TQ), axis=0, keepdims=True)

    def search16(ref, need):
        def bit_step(b, cur):
            cand = cur + jnp.left_shift(jnp.int32(1), 15 - b)
            return jnp.where(count16(ref, cand) >= need, cand, cur)

        return lax.fori_loop(0, 16, bit_step, jnp.full((1, TQ), min16, jnp.int32))

    thr_hi = search16(hi16, topk)
    n_above = jnp.where(thr_hi == 2 ** 15 - 1, 0, count16(hi16, jnp.minimum(thr_hi + 1, 2 ** 15 - 1)))

    def fill_lo(c, carry):
        rs = pl.ds(pl.multiple_of(c * CR, CR), CR)
        k = sc[rs, :]
        lo = jnp.bitwise_and(k, 0xFFFF) + min16
        lo16[rs, :] = jnp.where(jnp.right_shift(k, 16) == thr_hi, lo, min16).astype(jnp.int16)
        return carry

    lax.fori_loop(0, ntrips, fill_lo, 0)
    thr_lo = search16(lo16, topk - n_above)
    thr = thr_hi * 2 ** 16 + (thr_lo - min16)
    thr_b = bc(thr)
    n_ge = count(lambda k, s: k >= thr_b)
    has_ties = jnp.max(jnp.where(thr > int_min, n_ge, 0)) > topk

    def write(sel):
        def body(c, carry):
            rs = pl.ds(pl.multiple_of(c * CR, CR), CR)
            s_pos = c * CR + row
            keep = sel(sc[rs, :], s_pos) & (s_pos <= t_pos)
            o_ref[0, rs, :] = jnp.where(keep, 0.0, MASKED).astype(o_ref.dtype)
            return carry

        lax.fori_loop(0, ntrips, body, 0)

    @pl.when(jnp.logical_not(has_ties))
    def _():
        write(lambda k, s_pos: k >= thr)

    @pl.when(has_ties)
    def _():
        need = bc(topk - count(lambda k, s: k > thr_b))

        def idx_step(b, lim):
            cand = lim + jnp.left_shift(jnp.int32(1), idx_bits - 1 - b)
            cand_b = bc(cand)
            n_before = count(lambda k, s: (k == thr_b) & (s < cand_b))
            return jnp.where(n_before < need[0, 0:1], cand, lim)

        lim = lax.fori_loop(0, idx_bits, idx_step, jnp.zeros((1, TQ), jnp.int32))
        write(lambda k, s_pos: (k > thr) | ((k == thr) & (s_pos <= lim)))

    def clear(c, carry):
        o_ref[0, pl.ds(pl.multiple_of(c * CR, CR), CR), :] = jnp.full((CR, TQ), MASKED, o_ref.dtype)
        return carry

    lax.fori_loop(ntrips, S // CR, clear, 0)


def _index_mask(qi_g, ki_t, wi, batch, seq, topk, TQ=256, TS=128):
    G, n, gw = qi_g.shape
    DI = ki_t.shape[1]
    HG = gw // DI
    NH = G * HG
    TQ = min(TQ, seq)
    TS = min(TS, TQ)
    nq = seq // TQ
    GW = 2 if nq % 2 == 0 else 1
    idx_bits = max(1, int(seq - 1).bit_length())
    return pl.pallas_call(
        functools.partial(_index_mask_kernel, TQ=TQ, TS=TS, S=seq, G=G, HG=HG, DI=DI, GW=GW, topk=topk,
                          idx_bits=idx_bits),
        grid=(batch, nq),
        in_specs=[
            pl.BlockSpec((G, TQ, gw), lambda b, i: (0, b * nq + i, 0)),
            pl.BlockSpec((1, DI, seq), lambda b, i: (b, 0, 0)),
            pl.BlockSpec((TQ, NH), lambda b, i: (b * nq + i, 0)),
        ],
        out_specs=pl.BlockSpec((1, seq, TQ), lambda b, i: (b, 0, i)),
        out_shape=jax.ShapeDtypeStruct((batch, seq, seq), BF16),
        scratch_shapes=[
            pltpu.VMEM((seq, TQ), jnp.int32),
            pltpu.VMEM((seq, TQ), jnp.int16),
            pltpu.VMEM((seq, TQ), jnp.int16),
            pltpu.VMEM((NH, TQ, TS), F32),
            pltpu.VMEM((GW * TQ // TS, HG * DI, HG * TS), BF16),
        ],
        compiler_params=_params("parallel", "parallel"),
        name="index_mask",
    )(qi_g, ki_t, wi)


def _attn_kernel(ii_ref, jj_ref, qt_ref, k_ref, vt_ref, mask_ref, bias_ref, bfar_ref, o_ref,
                 m_scr, r_scr, a_scr, z_scr, acc_scr, *, H, DH, VA, NEAR, KT, T, c2):
    pair = pl.program_id(1)
    i = ii_ref[pair]
    jg = jj_ref[pair]

    @pl.when(jg == 0)
    def _():
        m_scr[...] = jnp.full_like(m_scr, MASKED)
        acc_scr[...] = jnp.zeros_like(acc_scr)

    def key_tile(sub, carry):
        j = jg * KT + sub
        ks = pl.ds(pl.multiple_of(sub * T, T), T)

        def scores(near):
            mask = mask_ref[0, ks, :].astype(F32)
            for h in range(H):
                hs = slice(h * DH, (h + 1) * DH)
                z = jnp.dot(k_ref[ks, hs], qt_ref[hs, :], preferred_element_type=F32) * c2 + mask
                m_old = m_scr[h]
                if near is not None:
                    z = z + bias_ref[h, near]
                    r = jnp.maximum(m_old, jnp.max(z, axis=0, keepdims=True)).astype(BF16)
                    m_new = r.astype(F32)
                else:
                    shift = bfar_ref[h]
                    r = (jnp.maximum(m_old, jnp.max(z, axis=0, keepdims=True) + shift) - shift).astype(BF16)
                    m_new = r.astype(F32) + shift
                z_scr[h] = z.astype(BF16)
                a_scr[h] = jnp.exp2(m_old - m_new)
                r_scr[h] = jnp.broadcast_to(r, (BF16_ROWS, T))
                m_scr[h] = m_new

        for d in range(NEAR):
            pl.when(i - j == d)(functools.partial(scores, d))
        pl.when(i - j >= NEAR)(functools.partial(scores, None))

        for h in range(H):
            z3 = z_scr[h].reshape(T // BF16_ROWS, BF16_ROWS, T)
            p = jnp.exp2(z3 - r_scr[h][None]).reshape(T, T)
            pv = jnp.dot(vt_ref[h * VA:(h + 1) * VA, ks], p, preferred_element_type=F32)
            acc_scr[h] = a_scr[h] * acc_scr[h] + pv
        return carry

    lax.fori_loop(0, jnp.minimum(KT, i - jg * KT + 1), key_tile, 0)

    @pl.when(jg * KT + KT > i)
    def _():
        for h in range(H):
            a = acc_scr[h]
            o_ref[:, h * DH:(h + 1) * DH] = (a[:DH] / a[DH:DH + 1]).T.astype(o_ref.dtype)


def _attention(q_t, k, vt_aug, mask_t, bias_t, bias_far, batch, seq, n_heads, T):
    width, n = q_t.shape
    DH = width // n_heads
    VA = vt_aug.shape[0] // n_heads
    nt = seq // T
    NEAR = bias_t.shape[1]
    c2 = (DH ** -0.5) * math.log2(math.e)
    KT = next(c for c in (4, 2, 1) if nt % c == 0)
    ng = nt // KT
    pairs = [(i, jg) for i in range(nt) for jg in range(i // KT + 1)]
    ii = jnp.asarray([p[0] for p in pairs], jnp.int32)
    jj = jnp.asarray([p[1] for p in pairs], jnp.int32)
    stat = pltpu.VMEM((n_heads, 1, T), F32)
    grid_spec = pltpu.PrefetchScalarGridSpec(
        num_scalar_prefetch=2,
        grid=(batch, len(pairs)),
        in_specs=[
            pl.BlockSpec((width, T), lambda b, p, ii, jj: (0, b * nt + ii[p])),
            pl.BlockSpec((KT * T, width), lambda b, p, ii, jj: (b * ng + jj[p], 0)),
            pl.BlockSpec((n_heads * VA, KT * T), lambda b, p, ii, jj: (0, b * ng + jj[p])),
            pl.BlockSpec((1, KT * T, T), lambda b, p, ii, jj: (b, jj[p], ii[p])),
            pl.BlockSpec(bias_t.shape, lambda b, p, ii, jj: (0, 0, 0, 0)),
            pl.BlockSpec(bias_far.shape, lambda b, p, ii, jj: (0, 0, 0)),
        ],
        out_specs=pl.BlockSpec((T, width), lambda b, p, ii, jj: (b * nt + ii[p], 0)),
        scratch_shapes=[stat, pltpu.VMEM((n_heads, BF16_ROWS, T), BF16), stat,
                        pltpu.VMEM((n_heads, T, T), BF16), pltpu.VMEM((n_heads, VA, T), F32)],
    )
    return pl.pallas_call(
        functools.partial(_attn_kernel, H=n_heads, DH=DH, VA=VA, NEAR=NEAR, KT=KT, T=T, c2=c2),
        grid_spec=grid_spec,
        out_shape=jax.ShapeDtypeStruct((n, width), BF16),
        compiler_params=_params("parallel", "arbitrary"),
        name="masked_attention",
    )(ii, jj, q_t, k, vt_aug, mask_t, bias_t, bias_far)


def _t5_bucket(n):
    max_exact = N_BUCKETS // 2
    nf = jnp.maximum(n, 1).astype(F32)
    large = max_exact + (jnp.log(nf / max_exact) / math.log(MAX_DISTANCE / max_exact)
                         * (N_BUCKETS - max_exact)).astype(jnp.int32)
    large = jnp.minimum(large, N_BUCKETS - 1)
    return jnp.where(n < max_exact, n, large)


def _bias_tiles(rel_bias, T, near):
    s = jnp.arange(T, dtype=jnp.int32)[:, None]
    t = jnp.arange(T, dtype=jnp.int32)[None, :]
    rb = rel_bias.astype(F32) * math.log2(math.e)
    tiles = []
    for d in range(near):
        bucket = _t5_bucket(jnp.maximum(d * T + t - s, 0))
        onehot = bucket[None, :, :] == jnp.arange(N_BUCKETS, dtype=jnp.int32)[:, None, None]
        tiles.append(jnp.sum(jnp.where(onehot[:, None], rb[:, :, None, None], 0.0), axis=0))
    far = jnp.broadcast_to(rb[N_BUCKETS - 1][:, None, None], (rb.shape[1], 1, T))
    return jnp.stack(tiles, axis=1), far


def kernel(x, norm1_g, w_in, b_gate, conv_w, conv_bias, conv_ln_g, conv_ln_b, w_conv_out, w_attn_out,
           rel_bias, w_o, norm2_g, w_ff1, w_ff2, normf_g):
    B, S, D = x.shape
    N = B * S
    depth = w_in.shape[0]
    C = conv_w.shape[-1]
    A = w_attn_out.shape[1]
    H = rel_bias.shape[1]
    DH = A // H
    NH, DI = IDX_HEADS, IDX_DIM
    HG = MXU_DIM // DI
    topk = min(TOPK_MAX, S // 4)
    off_q = 2 * C
    off_iq = off_q + 3 * A
    off_ik = off_iq + NH * DI
    off_iw = off_ik + DI
    off_gate = off_iw + NH
    T_ATT = min(256, S)
    near = -(-(MAX_DISTANCE + T_ATT - 1) // T_ATT)
    bias_t, bias_far = _bias_tiles(rel_bias, T_ATT, near)
    sigmoid_bias = lambda acc, b: _sigmoid(acc + b)
    relu2 = lambda acc: jnp.square(jnp.maximum(acc, 0.0))
    add = lambda acc, r: acc + r

    h = x.reshape(N, D)
    for l in range(depth):
        wl = w_in[l]
        u = _rmsnorm(h, norm1_g[l], BF16)
        wb16 = wl.astype(BF16)
        p_conv = _matmul(u, wb16, F32, wcols=(0, off_q), name="proj_conv")
        q_t = _matmul_t(u, wb16, BF16, wcols=(off_q, A), name="proj_q")
        k = _matmul(u, wb16, BF16, wcols=(off_q + A, A), name="proj_k")
        vt_aug = _matmul_t(u, wb16, BF16, wcols=(off_q + 2 * A, A), DH=DH, ones_rows=BF16_ROWS, name="proj_v")
        qi_g, kw = _matmul(u, wb16, BF16, wcols=(off_iq, NH * DI), group_width=HG * DI,
                           side=(off_ik, LANES, F32), name="proj_qi")
        gates = _matmul(u, wb16[:, off_gate:], BF16, epilogue=sigmoid_bias,
                        extras=[("row", b_gate[l].reshape(1, -1))], name="proj_gates")

        cf = _conv_branch(p_conv, conv_w[l], conv_bias[l], conv_ln_g[l], conv_ln_b[l], B, S)

        ki_t = jnp.transpose(kw[:, :DI].astype(BF16).reshape(B, S, DI), (0, 2, 1))
        wi = kw[:, DI:DI + NH] * (NH ** -0.5) * (DI ** -0.5)
        mask_t = _index_mask(qi_g, ki_t, wi, B, S, topk)
        o = _attention(q_t, k, vt_aug, mask_t, bias_t, bias_far, B, S, H, T_ATT)

        mixed = _gated_mix(cf, w_conv_out[l].astype(BF16), o, w_attn_out[l].astype(BF16), gates, BF16)
        h = _matmul(mixed, w_o[l].astype(BF16), F32, epilogue=add, extras=[("tile", h, 0)], name="w_o")
        u2 = _rmsnorm(h, norm2_g[l], BF16)
        act = _matmul(u2, w_ff1[l], BF16, epilogue=relu2, name="ff1")
        h = _matmul(act, w_ff2[l].astype(BF16), F32, epilogue=add, extras=[("tile", h, 0)], tn=512, name="ff2")
    return _rmsnorm(h, normf_g, F32).reshape(B, S, D)
```
